```python
import math
import jax, jax.numpy as jnp
from jax import lax
import numpy as np

D_MODEL = 4096
BATCH = 4
SEQ = 2048
DEPTH = 4
DEC_BATCH = 8
DEC_SEQ = 8
PAST_LEN = 8192
PAGE_SIZE = 128

HEAD_DIM = 128
ATT_WIDTH = 3 * D_MODEL // 8
CONV_CH = D_MODEL - 2 * ATT_WIDTH
DIFF_HEADS = ATT_WIDTH // (2 * HEAD_DIM)
SB_HEADS = ATT_WIDTH // HEAD_DIM
IN_COLS = 2 * CONV_CH + 6 * ATT_WIDTH
CONV_K = 31
N_EXPERTS = 32
TOP_K = 4
EXPERT_DIM = 3 * D_MODEL // 16
SWIGLU_LIMIT = 7.0
SWIGLU_ALPHA = 1.702
Q_BLOCK = 128
N_MOD = 6
EPS = 1e-6

kernel_name = "hybrid_conv_diff_stickbreak_moe_step"


def rms_norm(x, g):
    xf = x.astype(jnp.float32)
    y = xf * lax.rsqrt(jnp.mean(xf * xf, axis=-1, keepdims=True) + EPS)
    return (y * g.astype(jnp.float32)).astype(x.dtype)


def conv_module(u, conv_state, conv_w, conv_b, ln_g, ln_b):
    a, b = jnp.split(u, 2, axis=-1)
    g = a * jax.nn.sigmoid(b)
    ext = jnp.concatenate([conv_state.astype(g.dtype), g], axis=1)
    y = lax.conv_general_dilated(ext, conv_w[:, None, :].astype(ext.dtype), (1,), 'VALID',
                                 dimension_numbers=('NWC', 'WIO', 'NWC'),
                                 feature_group_count=CONV_CH) + conv_b
    yf = y.astype(jnp.float32)
    mu = jnp.mean(yf, axis=-1, keepdims=True)
    var = jnp.mean(jnp.square(yf - mu), axis=-1, keepdims=True)
    yn = (yf - mu) * lax.rsqrt(var + EPS) * ln_g.astype(jnp.float32) + ln_b.astype(jnp.float32)
    return jax.nn.silu(yn).astype(u.dtype), ext[:, -(CONV_K - 1):]


def diff_attention(q, k, v, q_pos, k_pos, lam, slopes):
    bsz, tq = q.shape[:2]
    s = jnp.einsum('bqhd,bkhd->bhqk', q, k, preferred_element_type=jnp.float32) * (HEAD_DIM ** -0.5)
    dist = (q_pos[:, None] - k_pos[None, :]).astype(jnp.float32)
    s = s - slopes[None, :, None, None] * dist[None, None]
    s = jnp.where((dist >= 0)[None, None], s, -jnp.inf)
    p = jax.nn.softmax(s, axis=-1).reshape(bsz, DIFF_HEADS, 2, tq, k.shape[1])
    a = p[:, :, 0] - lam * p[:, :, 1]
    return jnp.einsum('bhqk,bkhe->bqhe', a.astype(v.dtype), v)


def stick_breaking(q, k, v, q_pos, k_pos):
    z = jnp.einsum('bqhd,bkhd->bhqk', q, k, preferred_element_type=jnp.float32) * (HEAD_DIM ** -0.5)
    mask = (k_pos[None, :] < q_pos[:, None])[None, None]
    log_not = jnp.where(mask, -jax.nn.softplus(z), 0.0)
    between = lax.cumsum(log_not, axis=3, reverse=True) - log_not
    a = jnp.where(mask, jnp.exp(jax.nn.log_sigmoid(z) + between), 0.0)
    return jnp.einsum('bhqk,bkhd->bqhd', a.astype(v.dtype), v)


def over_query_blocks(fn, q, q_pos):
    bsz, t = q.shape[:2]
    if t % Q_BLOCK != 0:
        return fn(q, q_pos)
    nb = t // Q_BLOCK
    qb = q.reshape(bsz, nb, Q_BLOCK, *q.shape[2:]).swapaxes(0, 1)
    pb = q_pos.reshape(nb, Q_BLOCK)
    out = lax.map(lambda args: fn(args[0], args[1]), (qb, pb))
    return out.swapaxes(0, 1).reshape(bsz, t, *out.shape[3:])


def moe(h, router_w, router_b, w_gu, b_gu, w_down, b_down):
    logits = jnp.matmul(h, router_w, preferred_element_type=jnp.float32) + router_b.astype(jnp.float32)
    top_val, top_idx = lax.top_k(logits, TOP_K)
    gates = jax.nn.softmax(top_val, axis=-1)
    combine = jnp.einsum('nk,nke->ne', gates, jax.nn.one_hot(top_idx, N_EXPERTS, dtype=jnp.float32))
    out = jnp.zeros(h.shape, jnp.float32)
    for e in range(N_EXPERTS):
        gu = h @ w_gu[e] + b_gu[e]
        gate = jnp.minimum(gu[:, :EXPERT_DIM], SWIGLU_LIMIT)
        up = jnp.clip(gu[:, EXPERT_DIM:], -SWIGLU_LIMIT, SWIGLU_LIMIT)
        act = (up + 1.0) * gate * jax.nn.sigmoid(SWIGLU_ALPHA * gate)
        out = out + combine[:, e:e + 1] * (act @ w_down[e] + b_down[e]).astype(jnp.float32)
    return out.astype(h.dtype)


def gather_pages(cache, layer, page_table):
    g = cache[layer, page_table]
    return g.reshape(g.shape[0], g.shape[1] * g.shape[2], *g.shape[3:])


def trunk_layer(x, mod, conv_state, past, q_pos, lam_init, slopes, w):
    (g1, g2, wi, wo, cw, cb, lng, lnb, qng, kng, lam_p, sg,
     rw, rb, wgu, bgu, wd, bd) = w
    bsz, t, _ = x.shape
    sh1, sc1, gt1, sh2, sc2, gt2 = (mod[:, i, None, :] for i in range(N_MOD))
    h = rms_norm(x, g1) * (1 + sc1) + sh1
    proj = h @ wi
    cut = [2 * CONV_CH + i * ATT_WIDTH for i in range(6)]
    u_conv, dq, dk, dv, sq, sk, sv = jnp.split(proj, cut, axis=-1)
    y_conv, new_conv = conv_module(u_conv, conv_state, cw, cb, lng, lnb)
    dq = rms_norm(dq.reshape(bsz, t, 2 * DIFF_HEADS, HEAD_DIM), qng)
    dk = rms_norm(dk.reshape(bsz, t, 2 * DIFF_HEADS, HEAD_DIM), kng)
    dv = dv.reshape(bsz, t, DIFF_HEADS, 2 * HEAD_DIM)
    sq, sk, sv = (a.reshape(bsz, t, SB_HEADS, HEAD_DIM) for a in (sq, sk, sv))
    if past is None:
        ak_d, av_d, ak_s, av_s, k_pos = dk, dv, sk, sv, q_pos
    else:
        pk_d, pv_d, pk_s, pv_s = past
        ak_d = jnp.concatenate([pk_d.astype(dk.dtype), dk], axis=1)
        av_d = jnp.concatenate([pv_d.astype(dv.dtype), dv], axis=1)
        ak_s = jnp.concatenate([pk_s.astype(sk.dtype), sk], axis=1)
        av_s = jnp.concatenate([pv_s.astype(sv.dtype), sv], axis=1)
        k_pos = jnp.arange(pk_d.shape[1] + t, dtype=jnp.int32)
    lp = lam_p.astype(jnp.float32)
    lam = jnp.exp(jnp.sum(lp[0] * lp[1])) - jnp.exp(jnp.sum(lp[2] * lp[3])) + lam_init
    o_d = over_query_blocks(lambda qb, pb: diff_attention(qb, ak_d, av_d, pb, k_pos, lam, slopes), dq, q_pos)
    o_d = rms_norm(o_d, sg) * (1.0 - lam_init)
    o_s = over_query_blocks(lambda qb, pb: stick_breaking(qb, ak_s, av_s, pb, k_pos), sq, q_pos)
    mixed = jnp.concatenate([y_conv, o_d.reshape(bsz, t, ATT_WIDTH), o_s.reshape(bsz, t, ATT_WIDTH)], axis=-1) @ wo
    x = x + gt1 * mixed
    h2 = rms_norm(x, g2) * (1 + sc2) + sh2
    x = x + gt2 * moe(h2.reshape(bsz * t, D_MODEL), rw, rb, wgu, bgu, wd, bd).reshape(bsz, t, D_MODEL)
    return x, (dk, dv, sk, sv, new_conv)


def setup_inputs(seed: int = 0) -> dict:
    key = jax.random.key(seed)
    k = jax.random.split(key, 32)
    f32 = jnp.float32
    n_pages = PAST_LEN // PAGE_SIZE
    n_used = DEC_BATCH * n_pages
    n_pool = n_used + n_used // 4
    nrm = lambda kk, shape, scale=1.0: jax.random.normal(kk, shape, f32) * scale
    gain = lambda kk, shape: 1.0 + 0.02 * jax.random.normal(kk, shape, f32)
    return {
        "x_prompt": nrm(k[0], (BATCH, SEQ, D_MODEL)),
        "x_sample": nrm(k[1], (DEC_BATCH, DEC_SEQ, D_MODEL)),
        "cache_diff_k": nrm(k[2], (DEPTH, n_pool, PAGE_SIZE, 2 * DIFF_HEADS, HEAD_DIM)),
        "cache_diff_v": nrm(k[3], (DEPTH, n_pool, PAGE_SIZE, DIFF_HEADS, 2 * HEAD_DIM)),
        "cache_sb_k": nrm(k[4], (DEPTH, n_pool, PAGE_SIZE, SB_HEADS, HEAD_DIM)),
        "cache_sb_v": nrm(k[5], (DEPTH, n_pool, PAGE_SIZE, SB_HEADS, HEAD_DIM)),
        "state_conv": nrm(k[6], (DEPTH, DEC_BATCH, CONV_K - 1, CONV_CH), 0.5),
        "page_table": jax.random.permutation(k[7], n_pool)[:n_used].reshape(DEC_BATCH, n_pages).astype(jnp.int32),
        "c_prompt": nrm(k[8], (BATCH, D_MODEL)),
        "c_sample": nrm(k[9], (DEC_BATCH, D_MODEL)),
        "ada_w": nrm(k[10], (D_MODEL, N_MOD * D_MODEL), 0.5 * D_MODEL ** -0.5),
        "ada_b": nrm(k[11], (N_MOD * D_MODEL,), 0.02),
        "ada_table": nrm(k[12], (DEPTH, N_MOD, D_MODEL), 0.1),
        "norm1_g": gain(k[13], (DEPTH, D_MODEL)),
        "norm2_g": gain(k[14], (DEPTH, D_MODEL)),
        "w_in": nrm(k[15], (DEPTH, D_MODEL, IN_COLS), D_MODEL ** -0.5),
        "w_out": nrm(k[16], (DEPTH, D_MODEL, D_MODEL), D_MODEL ** -0.5),
        "conv_w": nrm(k[17], (DEPTH, CONV_K, CONV_CH), CONV_K ** -0.5),
        "conv_b": nrm(k[18], (DEPTH, CONV_CH), 0.02),
        "conv_ln_g": gain(k[19], (DEPTH, CONV_CH)),
        "conv_ln_b": nrm(k[20], (DEPTH, CONV_CH), 0.02),
        "diff_qn_g": gain(k[21], (DEPTH, HEAD_DIM)),
        "diff_kn_g": gain(k[22], (DEPTH, HEAD_DIM)),
        "diff_lambda": nrm(k[23], (DEPTH, 4, HEAD_DIM), 0.1),
        "diff_subln_g": gain(k[24], (DEPTH, 2 * HEAD_DIM)),
        "router_w": nrm(k[25], (DEPTH, D_MODEL, N_EXPERTS), D_MODEL ** -0.5),
        "router_b": nrm(k[26], (DEPTH, N_EXPERTS), 0.01),
        "moe_w_gate_up": nrm(k[27], (DEPTH, N_EXPERTS, D_MODEL, 2 * EXPERT_DIM), D_MODEL ** -0.5),
        "moe_b_gate_up": nrm(k[28], (DEPTH, N_EXPERTS, 2 * EXPERT_DIM), 0.02),
        "moe_w_down": nrm(k[29], (DEPTH, N_EXPERTS, EXPERT_DIM, D_MODEL), EXPERT_DIM ** -0.5),
        "moe_b_down": nrm(k[30], (DEPTH, N_EXPERTS, D_MODEL), 0.02),
    }


def reference(x_prompt, x_sample, cache_diff_k, cache_diff_v, cache_sb_k, cache_sb_v, state_conv,
              page_table, c_prompt, c_sample, ada_w, ada_b, ada_table, norm1_g, norm2_g, w_in, w_out,
              conv_w, conv_b, conv_ln_g, conv_ln_b, diff_qn_g, diff_kn_g, diff_lambda, diff_subln_g,
              router_w, router_b, moe_w_gate_up, moe_b_gate_up, moe_w_down, moe_b_down):
    past_len = page_table.shape[1] * PAGE_SIZE
    pos_p = jnp.arange(x_prompt.shape[1], dtype=jnp.int32)
    pos_s = past_len + jnp.arange(x_sample.shape[1], dtype=jnp.int32)
    slopes = jnp.repeat(2.0 ** (-8.0 * (jnp.arange(DIFF_HEADS, dtype=jnp.float32) + 1.0) / DIFF_HEADS), 2)
    mod_p = (jax.nn.silu(c_prompt) @ ada_w + ada_b).reshape(c_prompt.shape[0], N_MOD, D_MODEL)
    mod_s = (jax.nn.silu(c_sample) @ ada_w + ada_b).reshape(c_sample.shape[0], N_MOD, D_MODEL)
    conv0 = jnp.zeros((x_prompt.shape[0], CONV_K - 1, CONV_CH), x_prompt.dtype)
    xp, xs = x_prompt, x_sample
    states_p, states_s = [], []
    for l in range(DEPTH):
        lam_init = 0.8 - 0.6 * math.exp(-0.3 * l)
        w = (norm1_g[l], norm2_g[l], w_in[l], w_out[l], conv_w[l], conv_b[l], conv_ln_g[l], conv_ln_b[l],
             diff_qn_g[l], diff_kn_g[l], diff_lambda[l], diff_subln_g[l], router_w[l], router_b[l],
             moe_w_gate_up[l], moe_b_gate_up[l], moe_w_down[l], moe_b_down[l])
        xp, st_p = trunk_layer(xp, mod_p + ada_table[l][None], conv0, None, pos_p, lam_init, slopes, w)
        past = (gather_pages(cache_diff_k, l, page_table), gather_pages(cache_diff_v, l, page_table),
                gather_pages(cache_sb_k, l, page_table), gather_pages(cache_sb_v, l, page_table))
        xs, st_s = trunk_layer(xs, mod_s + ada_table[l][None], state_conv[l], past, pos_s, lam_init, slopes, w)
        states_p.append(st_p)
        states_s.append(st_s)
    new_diff_k_p, new_diff_v_p, new_sb_k_p, new_sb_v_p, new_conv_p = [jnp.stack(z) for z in zip(*states_p)]
    new_diff_k_s, new_diff_v_s, new_sb_k_s, new_sb_v_s, new_conv_s = [jnp.stack(z) for z in zip(*states_s)]
    return (xp, xs, new_diff_k_p, new_diff_v_p, new_sb_k_p, new_sb_v_p, new_conv_p,
            new_diff_k_s, new_diff_v_s, new_sb_k_s, new_sb_v_s, new_conv_s)
```

```python
import functools
import math

import jax
import jax.numpy as jnp
from jax import lax
from jax.experimental import pallas as pl
from jax.experimental.pallas import tpu as pltpu

F32 = jnp.float32
BF16 = jnp.bfloat16

HEAD_DIM = 128
N_MOD = 6
CONV_K = 31
N_EXPERTS = 32
TOP_K = 4
SWIGLU_LIMIT = 7.0
SWIGLU_ALPHA = 1.702
EPS = 1e-6
PAGE_SIZE = 128
SCALE = HEAD_DIM ** -0.5
NEG_INF = float("-inf")

LANES = 128
VMEM_LIMIT = 56 * 1024 * 1024

MM_TM = 1024
MM_TN = 512
NORM_TT = 256
CONV_TT = 256
CONV_RB = 32
ATT_TQ = 256
MOE_TM = 256
MOE_TC = 384
COMBINE_TT = 128


def _params(sem, vmem=VMEM_LIMIT):
    return pltpu.CompilerParams(dimension_semantics=sem, vmem_limit_bytes=vmem)


def _dot(a, b):
    return jnp.dot(a, b, preferred_element_type=F32)


def _dot_nt(a, b):
    return lax.dot_general(a, b, (((1,), (1,)), ((), ())), preferred_element_type=F32)


def _rms(x, gain):
    y = x * lax.rsqrt(jnp.mean(x * x, axis=-1, keepdims=True) + EPS)
    return y * gain


def _mm_body(*refs, k_splits, has_bias, has_gain, has_resid):
    n_a = len(k_splits)
    a_refs = refs[:n_a]
    w_ref = refs[n_a]
    pos = n_a + 1
    bias_ref = gain_ref = x_ref = gate_ref = None
    if has_bias:
        bias_ref = refs[pos]
        pos += 1
    if has_gain:
        gain_ref = refs[pos]
        pos += 1
    if has_resid:
        x_ref, gate_ref = refs[pos], refs[pos + 1]
        pos += 2
    o_ref = refs[pos]

    acc = None
    off = 0
    for a_ref, kk in zip(a_refs, k_splits):
        part = _dot(a_ref[...].astype(BF16), w_ref[off:off + kk, :].astype(BF16))
        acc = part if acc is None else acc + part
        off += kk
    if has_bias:
        acc = acc + bias_ref[...]
    if has_gain:
        g = gain_ref[...]
        pieces = []
        for c in range(acc.shape[1] // HEAD_DIM):
            pieces.append(_rms(acc[:, c * HEAD_DIM:(c + 1) * HEAD_DIM], g))
        acc = jnp.concatenate(pieces, axis=1)
    if has_resid:
        acc = x_ref[...] + gate_ref[...] * acc
    o_ref[...] = acc.astype(o_ref.dtype)


def _matmul(a_list, w, *, col_off=0, n_cols=None, bias=None, gain=None, resid=None,
            gate=None, gate_rows_per_block=None, out_dtype=F32, name="mm"):
    m = a_list[0].shape[0]
    k_splits = tuple(a.shape[1] for a in a_list)
    k_total = sum(k_splits)
    assert w.shape[0] == k_total
    n_cols = w.shape[1] - col_off if n_cols is None else n_cols
    tm = min(MM_TM, m)
    tn = MM_TN
    assert m % tm == 0 and n_cols % tn == 0 and col_off % tn == 0
    joff = col_off // tn
    grid = (m // tm, n_cols // tn)

    in_specs = [pl.BlockSpec((tm, kk), lambda i, j: (i, 0)) for kk in k_splits]
    in_specs.append(pl.BlockSpec((k_total, tn), lambda i, j: (0, j + joff)))
    args = list(a_list) + [w]
    if bias is not None:
        in_specs.append(pl.BlockSpec((1, tn), lambda i, j: (0, j)))
        args.append(bias.reshape(1, n_cols))
    if gain is not None:
        in_specs.append(pl.BlockSpec((1, HEAD_DIM), lambda i, j: (0, 0)))
        args.append(gain.reshape(1, HEAD_DIM))
    if resid is not None:
        g_groups, g_rows, _ = gate.shape
        bpg = (m // tm) // g_groups
        in_specs.append(pl.BlockSpec((tm, tn), lambda i, j: (i, j)))
        in_specs.append(pl.BlockSpec((None, g_rows, tn), lambda i, j: (i // bpg, 0, j)))
        args += [resid, gate]

    body = functools.partial(_mm_body, k_splits=k_splits, has_bias=bias is not None,
                             has_gain=gain is not None, has_resid=resid is not None)
    return pl.pallas_call(
        body,
        grid=grid,
        in_specs=in_specs,
        out_specs=pl.BlockSpec((tm, tn), lambda i, j: (i, j)),
        out_shape=jax.ShapeDtypeStruct((m, n_cols), out_dtype),
        compiler_params=_params(("parallel", "arbitrary")),
        name=name,
    )(*args)


def _norm_body(x_ref, g_ref, mod_ref, tab_ref, *refs, i_shift, i_scale, with_router):
    if with_router:
        rw_ref, rb_ref, h_ref, idx_ref, gate_ref = refs
    else:
        (h_ref,) = refs
    x = x_ref[...]
    mod = mod_ref[...] + tab_ref[...]
    shift = mod[i_shift:i_shift + 1, :]
    scale = mod[i_scale:i_scale + 1, :]
    h = _rms(x, g_ref[...]) * (1.0 + scale) + shift
    h_ref[...] = h.astype(h_ref.dtype)
    if not with_router:
        return
    w = rw_ref[...]
    h_hi = h.astype(BF16)
    h_lo = (h - h_hi.astype(F32)).astype(BF16)
    w_hi = w.astype(BF16)
    w_lo = (w - w_hi.astype(F32)).astype(BF16)
    logits = _dot(h_hi, w_hi) + (_dot(h_hi, w_lo) + _dot(h_lo, w_hi)) + rb_ref[...]
    tt = logits.shape[0]
    e_iota = lax.broadcasted_iota(jnp.int32, (tt, N_EXPERTS), 1).astype(F32)
    lane = lax.broadcasted_iota(jnp.int32, (tt, LANES), 1)
    idx_out = jnp.zeros((tt, LANES), jnp.int32)
    val_out = jnp.zeros((tt, LANES), F32)
    work = logits
    vals = []
    for k in range(TOP_K):
        v = jnp.max(work, axis=-1, keepdims=True)
        sel = jnp.min(jnp.where(work == v, e_iota, float(N_EXPERTS)), axis=-1, keepdims=True)
        work = jnp.where(e_iota == sel, NEG_INF, work)
        idx_out = jnp.where(lane == k, sel.astype(jnp.int32), idx_out)
        vals.append(v)
    ex = [jnp.exp(v - vals[0]) for v in vals]
    den = ex[0] + ex[1] + ex[2] + ex[3]
    for k in range(TOP_K):
        val_out = jnp.where(lane == k, ex[k] / den, val_out)
    idx_ref[...] = idx_out
    gate_ref[...] = val_out


def _norm_mod(x, g, mod, table, *, i_shift, i_scale, out_dtype, tt, router=None, name="norm"):
    b, t, d = x.shape
    nt = t // tt
    assert t % tt == 0
    in_specs = [
        pl.BlockSpec((None, tt, d), lambda bi, ti: (bi, ti, 0)),
        pl.BlockSpec((1, d), lambda bi, ti: (0, 0)),
        pl.BlockSpec((None, N_MOD, d), lambda bi, ti: (bi, 0, 0)),
        pl.BlockSpec((N_MOD, d), lambda bi, ti: (0, 0)),
    ]
    args = [x, g.reshape(1, d), mod, table]
    row_spec = lambda w: pl.BlockSpec((tt, w), lambda bi, ti: (bi * nt + ti, 0))
    out_specs = [row_spec(d)]
    out_shape = [jax.ShapeDtypeStruct((b * t, d), out_dtype)]
    if router is not None:
        rw, rb = router
        in_specs += [pl.BlockSpec((d, N_EXPERTS), lambda bi, ti: (0, 0)),
                     pl.BlockSpec((1, N_EXPERTS), lambda bi, ti: (0, 0))]
        args += [rw, rb.reshape(1, N_EXPERTS)]
        out_specs += [row_spec(LANES), row_spec(LANES)]
        out_shape += [jax.ShapeDtypeStruct((b * t, LANES), jnp.int32),
                      jax.ShapeDtypeStruct((b * t, LANES), F32)]
    body = functools.partial(_norm_body, i_shift=i_shift, i_scale=i_scale,
                             with_router=router is not None)
    outs = pl.pallas_call(
        body, grid=(b, nt), in_specs=in_specs, out_specs=out_specs, out_shape=out_shape,
        compiler_params=_params(("parallel", "arbitrary")), name=name,
    )(*args)
    return outs if router is not None else outs[0]


HIST = CONV_K - 1
HIST_PAD = 32


def _conv_body(u_ref, st_ref, w_ref, b_ref, lg_ref, lb_ref, y_ref, ns_ref, ext_ref, *, tt, ch):
    ti = pl.program_id(1)
    lead = HIST_PAD - HIST

    @pl.when(ti == 0)
    def _():
        ext_ref[lead:HIST_PAD, :] = st_ref[...]

    @pl.when(ti > 0)
    def _():
        ext_ref[lead:HIST_PAD, :] = ext_ref[tt + lead:tt + HIST_PAD, :]

    u = u_ref[...]
    ext_ref[HIST_PAD:HIST_PAD + tt, :] = u[:, :ch] * jax.nn.sigmoid(u[:, ch:])

    rb = min(CONV_RB, tt)
    for r0 in range(0, tt, rb):
        acc = jnp.broadcast_to(b_ref[...], (rb, ch))
        for k in range(CONV_K):
            acc = acc + w_ref[k:k + 1, :] * ext_ref[r0 + lead + k:r0 + lead + k + rb, :]
        mu = jnp.mean(acc, axis=-1, keepdims=True)
        cen = acc - mu
        var = jnp.mean(cen * cen, axis=-1, keepdims=True)
        yn = cen * lax.rsqrt(var + EPS) * lg_ref[...] + lb_ref[...]
        y_ref[r0:r0 + rb, :] = (yn * jax.nn.sigmoid(yn)).astype(y_ref.dtype)

    @pl.when(ti == pl.num_programs(1) - 1)
    def _():
        ns_ref[...] = ext_ref[tt + lead:tt + HIST_PAD, :]


def _conv_module(u, state, cw, cb, lg, lb, *, batch, tt, out_dtype, name="conv"):
    n, c2 = u.shape
    ch = c2 // 2
    t = n // batch
    nt = t // tt
    assert t % tt == 0 and (nt == 1 or tt >= HIST)
    body = functools.partial(_conv_body, tt=tt, ch=ch)
    vec = lambda: pl.BlockSpec((1, ch), lambda bi, ti: (0, 0))
    return pl.pallas_call(
        body,
        grid=(batch, nt),
        in_specs=[
            pl.BlockSpec((tt, c2), lambda bi, ti: (bi * nt + ti, 0)),
            pl.BlockSpec((None, HIST, ch), lambda bi, ti: (bi, 0, 0)),
            pl.BlockSpec((CONV_K, ch), lambda bi, ti: (0, 0)),
            vec(), vec(), vec(),
        ],
        out_specs=[
            pl.BlockSpec((tt, ch), lambda bi, ti: (bi * nt + ti, 0)),
            pl.BlockSpec((None, HIST, ch), lambda bi, ti: (bi, 0, 0)),
        ],
        out_shape=[jax.ShapeDtypeStruct((n, ch), out_dtype),
                   jax.ShapeDtypeStruct((batch, HIST, ch), F32)],
        scratch_shapes=[pltpu.VMEM((HIST_PAD + tt, ch), F32)],
        compiler_params=_params(("parallel", "arbitrary")),
        name=name,
    )(u, state, cw, cb.reshape(1, ch), lg.reshape(1, ch), lb.reshape(1, ch))


def _subln(o, sg, lam_init):
    return _rms(o, sg) * (1.0 - lam_init)


def _diff_p_body(sc_ref, q_ref, k_ref, v_ref, sg_ref, o_ref,
                 kb_ref, vb_ref, m_ref, l_ref, acc_ref, o0_ref, *, tq, lam_init):
    hd = pl.program_id(1)
    t = q_ref.shape[0]
    nq = t // tq
    kb_ref[0] = k_ref[:, :HEAD_DIM].astype(BF16)
    kb_ref[1] = k_ref[:, HEAD_DIM:].astype(BF16)
    vb_ref[...] = v_ref[...].astype(BF16)
    lam = sc_ref[0]
    row = lax.broadcasted_iota(jnp.int32, (tq, tq), 0)
    col = lax.broadcasted_iota(jnp.int32, (tq, tq), 1)
    d0 = (row - col).astype(F32)

    def update(s, koff):
        m_old = m_ref[...]
        m_new = jnp.maximum(m_old, jnp.max(s, axis=-1, keepdims=True))
        alpha = jnp.exp(m_old - m_new)
        p = jnp.exp(s - m_new)
        l_ref[...] = alpha * l_ref[...] + jnp.sum(p, axis=-1, keepdims=True)
        acc_ref[...] = alpha * acc_ref[...] + _dot(p.astype(BF16), vb_ref[pl.ds(koff, tq), :])
        m_ref[...] = m_new

    def q_block(qi, carry):
        qoff = pl.multiple_of(qi * tq, tq)
        for sub in range(2):
            slope = sc_ref[1 + 2 * hd + sub]
            q = q_ref[pl.ds(qoff, tq), sub * HEAD_DIM:(sub + 1) * HEAD_DIM]
            m_ref[...] = jnp.full((tq, 1), NEG_INF, F32)
            l_ref[...] = jnp.zeros((tq, 1), F32)
            acc_ref[...] = jnp.zeros((tq, 2 * HEAD_DIM), F32)

            def k_block(kj, c):
                koff = pl.multiple_of(kj * tq, tq)
                s = _dot_nt(q, kb_ref[sub, pl.ds(koff, tq), :]) * SCALE
                dist = d0 + lax.convert_element_type((qi - kj) * tq, F32)
                update(s - slope * dist, koff)
                return c

            lax.fori_loop(0, qi, k_block, 0)
            s = _dot_nt(q, kb_ref[sub, pl.ds(qoff, tq), :]) * SCALE
            s = jnp.where(d0 >= 0, s - slope * d0, NEG_INF)
            update(s, qoff)
            o_sub = acc_ref[...] / l_ref[...]
            if sub == 0:
                o0_ref[...] = o_sub
            else:
                o = o0_ref[...] - lam * o_sub
                o_ref[pl.ds(qoff, tq), :] = _subln(o, sg_ref[...], lam_init).astype(o_ref.dtype)
        return carry

    lax.fori_loop(0, nq, q_block, 0)


def _diff_attn_prompt(q, k, v, scalars, sg, *, batch, lam_init, name="diff_p"):
    n, width = q.shape
    t = n // batch
    heads = width // (2 * HEAD_DIM)
    tq = min(ATT_TQ, t)
    hw = 2 * HEAD_DIM
    blk = lambda: pl.BlockSpec((t, hw), lambda b, h: (b, h))
    body = functools.partial(_diff_p_body, tq=tq, lam_init=lam_init)
    return pl.pallas_call(
        body,
        grid=(batch, heads),
        in_specs=[pl.BlockSpec(memory_space=pltpu.SMEM), blk(), blk(), blk(),
                  pl.BlockSpec((1, hw), lambda b, h: (0, 0))],
        out_specs=blk(),
        out_shape=jax.ShapeDtypeStruct((n, width), BF16),
        scratch_shapes=[pltpu.VMEM((2, t, HEAD_DIM), BF16), pltpu.VMEM((t, hw), BF16),
                        pltpu.VMEM((tq, 1), F32), pltpu.VMEM((tq, 1), F32),
                        pltpu.VMEM((tq, hw), F32), pltpu.VMEM((tq, hw), F32)],
        compiler_params=_params(("parallel", "arbitrary")),
        name=name,
    )(scalars, q, k, v, sg.reshape(1, hw))


def _softplus(z):
    return jnp.maximum(z, 0.0) + jnp.log1p(jnp.exp(-jnp.abs(z)))


def _suffix_matrix(n):
    r = lax.broadcasted_iota(jnp.int32, (n, n), 0)
    c = lax.broadcasted_iota(jnp.int32, (n, n), 1)
    return (r > c).astype(BF16)


def _suffix_sum(x, u):
    hi = x.astype(BF16)
    lo = (x - hi.astype(F32)).astype(BF16)
    return _dot(hi, u) + _dot(lo, u)


def _sb_p_body(q_ref, k_ref, v_ref, o_ref, kb_ref, vb_ref, c_ref, acc_ref, *, tq):
    t = q_ref.shape[0]
    nq = t // tq
    kb_ref[...] = k_ref[...].astype(BF16)
    vb_ref[...] = v_ref[...].astype(BF16)
    row = lax.broadcasted_iota(jnp.int32, (tq, tq), 0)
    col = lax.broadcasted_iota(jnp.int32, (tq, tq), 1)
    below = col < row
    u = _suffix_matrix(tq)

    def block(q, koff, mask):
        z = _dot_nt(q, kb_ref[pl.ds(koff, tq), :]) * SCALE
        sp = _softplus(z)
        log_not = -sp if mask is None else jnp.where(mask, -sp, 0.0)
        between = _suffix_sum(log_not, u) + c_ref[...]
        a = jnp.exp((z - sp) + between)
        if mask is not None:
            a = jnp.where(mask, a, 0.0)
        acc_ref[...] += _dot(a.astype(BF16), vb_ref[pl.ds(koff, tq), :])
        c_ref[...] += jnp.sum(log_not, axis=-1, keepdims=True)

    def q_block(qi, carry):
        qoff = pl.multiple_of(qi * tq, tq)
        q = q_ref[pl.ds(qoff, tq), :]
        c_ref[...] = jnp.zeros((tq, 1), F32)
        acc_ref[...] = jnp.zeros((tq, HEAD_DIM), F32)
        block(q, qoff, below)

        def k_block(jj, c):
            koff = pl.multiple_of((qi - 1 - jj) * tq, tq)
            block(q, koff, None)
            return c

        lax.fori_loop(0, qi, k_block, 0)
        o_ref[pl.ds(qoff, tq), :] = acc_ref[...].astype(o_ref.dtype)
        return carry

    lax.fori_loop(0, nq, q_block, 0)


def _sb_attn_prompt(q, k, v, *, batch, name="sb_p"):
    n, width = q.shape
    t = n // batch
    heads = width // HEAD_DIM
    tq = min(ATT_TQ, t)
    blk = lambda: pl.BlockSpec((t, HEAD_DIM), lambda b, h: (b, h))
    body = functools.partial(_sb_p_body, tq=tq)
    return pl.pallas_call(
        body,
        grid=(batch, heads),
        in_specs=[blk(), blk(), blk()],
        out_specs=blk(),
        out_shape=jax.ShapeDtypeStruct((n, width), BF16),
        scratch_shapes=[pltpu.VMEM((t, HEAD_DIM), BF16), pltpu.VMEM((t, HEAD_DIM), BF16),
                        pltpu.VMEM((tq, 1), F32), pltpu.VMEM((tq, HEAD_DIM), F32)],
        compiler_params=_params(("parallel", "arbitrary")),
        name=name,
    )(q, k, v)


def _diff_s_body(pt_ref, sc_ref, q_ref, kn_ref, vn_ref, kc_ref, vc_ref, sg_ref, o_ref,
                 m_ref, l_ref, acc_ref, *, n_sub, tnew, past_len, lam_init):
    j = pl.program_id(1)
    n_pages = pl.num_programs(1)
    row = lax.broadcasted_iota(jnp.int32, (tnew, PAGE_SIZE), 0)
    col = lax.broadcasted_iota(jnp.int32, (tnew, PAGE_SIZE), 1)
    q_all = q_ref[...] * SCALE

    def attend(h, k_h, v_h, dist, mask, first):
        q_h = q_all[:, h * HEAD_DIM:(h + 1) * HEAD_DIM].astype(BF16)
        s = _dot_nt(q_h, k_h.astype(BF16)) - sc_ref[1 + h] * dist
        if mask is not None:
            s = jnp.where(mask, s, NEG_INF)
        if first:
            m_new = jnp.max(s, axis=-1, keepdims=True)
            p = jnp.exp(s - m_new)
            l_ref[h] = jnp.sum(p, axis=-1, keepdims=True)
            acc_ref[h] = _dot(p.astype(BF16), v_h.astype(BF16))
        else:
            m_old = m_ref[h]
            m_new = jnp.maximum(m_old, jnp.max(s, axis=-1, keepdims=True))
            alpha = jnp.exp(m_old - m_new)
            p = jnp.exp(s - m_new)
            l_ref[h] = alpha * l_ref[h] + jnp.sum(p, axis=-1, keepdims=True)
            acc_ref[h] = alpha * acc_ref[h] + _dot(p.astype(BF16), v_h.astype(BF16))
        m_ref[h] = m_new

    @pl.when(j == 0)
    def _():
        d_new = (row - col).astype(F32)
        for h in range(n_sub):
            attend(h, kn_ref[:, h * HEAD_DIM:(h + 1) * HEAD_DIM],
                   vn_ref[:, (h // 2) * 2 * HEAD_DIM:(h // 2 + 1) * 2 * HEAD_DIM],
                   d_new, col <= row, True)

    d_page = (past_len + row - j * PAGE_SIZE - col).astype(F32)
    for h in range(n_sub):
        attend(h, kc_ref[:, h, :], vc_ref[:, h // 2, :], d_page, None, False)

    @pl.when(j == n_pages - 1)
    def _():
        lam = sc_ref[0]
        for hd in range(n_sub // 2):
            o0 = acc_ref[2 * hd] / l_ref[2 * hd]
            o1 = acc_ref[2 * hd + 1] / l_ref[2 * hd + 1]
            o = _subln(o0 - lam * o1, sg_ref[...], lam_init)
            o_ref[:, hd * 2 * HEAD_DIM:(hd + 1) * 2 * HEAD_DIM] = o.astype(o_ref.dtype)


def _pad_new(x, batch):
    n, w = x.shape
    t = n // batch
    x = x.reshape(batch, t, w)
    return jnp.pad(x, ((0, 0), (0, PAGE_SIZE - t), (0, 0))).reshape(batch * PAGE_SIZE, w)


def _diff_attn_sample(q, k_new, v_new, cache_k, cache_v, page_table, scalars, sg, *,
                      layer, lam_init, name="diff_s"):
    batch, n_pages = page_table.shape
    n, width = q.shape
    tnew = n // batch
    n_sub = width // HEAD_DIM
    hw = 2 * HEAD_DIM
    body = functools.partial(_diff_s_body, n_sub=n_sub, tnew=tnew,
                             past_len=n_pages * PAGE_SIZE, lam_init=lam_init)
    grid_spec = pltpu.PrefetchScalarGridSpec(
        num_scalar_prefetch=1,
        grid=(batch, n_pages),
        in_specs=[
            pl.BlockSpec(memory_space=pltpu.SMEM),
            pl.BlockSpec((tnew, width), lambda b, j, pt: (b, 0)),
            pl.BlockSpec((PAGE_SIZE, width), lambda b, j, pt: (b, 0)),
            pl.BlockSpec((PAGE_SIZE, width), lambda b, j, pt: (b, 0)),
            pl.BlockSpec((None, None, PAGE_SIZE, n_sub, HEAD_DIM),
                         lambda b, j, pt: (layer, pt[b, j], 0, 0, 0)),
            pl.BlockSpec((None, None, PAGE_SIZE, n_sub // 2, hw),
                         lambda b, j, pt: (layer, pt[b, j], 0, 0, 0)),
            pl.BlockSpec((1, hw), lambda b, j, pt: (0, 0)),
        ],
        out_specs=pl.BlockSpec((tnew, width), lambda b, j, pt: (b, 0)),
        scratch_shapes=[pltpu.VMEM((n_sub, tnew, 1), F32), pltpu.VMEM((n_sub, tnew, 1), F32),
                        pltpu.VMEM((n_sub, tnew, hw), F32)],
    )
    return pl.pallas_call(
        body, grid_spec=grid_spec,
        out_shape=jax.ShapeDtypeStruct((n, width), F32),
        compiler_params=_params(("parallel", "arbitrary")),
        name=name,
    )(page_table, scalars, q, _pad_new(k_new, batch), _pad_new(v_new, batch),
      cache_k, cache_v, sg.reshape(1, hw))


def _sb_s_body(pt_ref, q_ref, kn_ref, vn_ref, kc_ref, vc_ref, o_ref, c_ref, acc_ref, *,
               heads, tnew):
    j = pl.program_id(1)
    n_pages = pl.num_programs(1)
    row = lax.broadcasted_iota(jnp.int32, (tnew, PAGE_SIZE), 0)
    col = lax.broadcasted_iota(jnp.int32, (tnew, PAGE_SIZE), 1)
    u = _suffix_matrix(PAGE_SIZE)
    q_all = q_ref[...] * SCALE

    def block(get_k, get_v, mask):
        zs, sps, lns = [], [], []
        for h in range(heads):
            q_h = q_all[:, h * HEAD_DIM:(h + 1) * HEAD_DIM].astype(BF16)
            z = _dot_nt(q_h, get_k(h).astype(BF16))
            sp = _softplus(z)
            zs.append(z)
            sps.append(sp)
            lns.append(-sp if mask is None else jnp.where(mask, -sp, 0.0))
        ln_all = jnp.concatenate(lns, axis=0)
        between_all = _suffix_sum(ln_all, u)
        for h in range(heads):
            between = between_all[h * tnew:(h + 1) * tnew, :] + c_ref[h]
            a = jnp.exp((zs[h] - sps[h]) + between)
            if mask is not None:
                a = jnp.where(mask, a, 0.0)
            acc_ref[h] += _dot(a.astype(BF16), get_v(h).astype(BF16))
            c_ref[h] += jnp.sum(lns[h], axis=-1, keepdims=True)

    @pl.when(j == 0)
    def _():
        c_ref[...] = jnp.zeros(c_ref.shape, F32)
        acc_ref[...] = jnp.zeros(acc_ref.shape, F32)
        sl = lambda ref: (lambda h: ref[:, h * HEAD_DIM:(h + 1) * HEAD_DIM])
        block(sl(kn_ref), sl(vn_ref), col < row)

    block(lambda h: kc_ref[:, h, :], lambda h: vc_ref[:, h, :], None)

    @pl.when(j == n_pages - 1)
    def _():
        for h in range(heads):
            o_ref[:, h * HEAD_DIM:(h + 1) * HEAD_DIM] = acc_ref[h].astype(o_ref.dtype)


def _sb_attn_sample(q, k_new, v_new, cache_k, cache_v, page_table, *, layer, name="sb_s"):
    batch, n_pages = page_table.shape
    n, width = q.shape
    tnew = n // batch
    heads = width // HEAD_DIM
    body = functools.partial(_sb_s_body, heads=heads, tnew=tnew)
    page = lambda: pl.BlockSpec((None, None, PAGE_SIZE, heads, HEAD_DIM),
                                lambda b, j, pt: (layer, pt[b, n_pages - 1 - j], 0, 0, 0))
    grid_spec = pltpu.PrefetchScalarGridSpec(
        num_scalar_prefetch=1,
        grid=(batch, n_pages),
        in_specs=[
            pl.BlockSpec((tnew, width), lambda b, j, pt: (b, 0)),
            pl.BlockSpec((PAGE_SIZE, width), lambda b, j, pt: (b, 0)),
            pl.BlockSpec((PAGE_SIZE, width), lambda b, j, pt: (b, 0)),
            page(), page(),
        ],
        out_specs=pl.BlockSpec((tnew, width), lambda b, j, pt: (b, 0)),
        scratch_shapes=[pltpu.VMEM((heads, tnew, 1), F32),
                        pltpu.VMEM((heads, tnew, HEAD_DIM), F32)],
    )
    return pl.pallas_call(
        body, grid_spec=grid_spec,
        out_shape=jax.ShapeDtypeStruct((n, width), F32),
        compiler_params=_params(("parallel", "arbitrary")),
        name=name,
    )(page_table, q, _pad_new(k_new, batch), _pad_new(v_new, batch), cache_k, cache_v)


def _gather_body(src_ref, nv_ref, h_hbm, o_ref, buf_ref, sem, *, tg):
    i = pl.program_id(0)

    @pl.when(i < nv_ref[0])
    def _():
        base = i * tg

        def issue(r, c):
            tok = src_ref[base + r]
            pltpu.make_async_copy(h_hbm.at[pl.ds(tok, 1), :], buf_ref.at[pl.ds(r, 1), :], sem).start()
            return c

        lax.fori_loop(0, tg, issue, 0)
        pltpu.make_async_copy(h_hbm.at[pl.ds(0, tg), :], buf_ref, sem).wait()
        o_ref[...] = buf_ref[...].astype(o_ref.dtype)

    @pl.when(i >= nv_ref[0])
    def _():
        o_ref[...] = jnp.zeros(o_ref.shape, o_ref.dtype)


def _moe_gather(h, src_tok, n_valid, *, rows, name="moe_gather"):
    n, d = h.shape
    tg = MOE_TM
    body = functools.partial(_gather_body, tg=tg)
    grid_spec = pltpu.PrefetchScalarGridSpec(
        num_scalar_prefetch=2,
        grid=(rows // tg,),
        in_specs=[pl.BlockSpec(memory_space=pl.ANY)],
        out_specs=pl.BlockSpec((tg, d), lambda i, s, nv: (i, 0)),
        scratch_shapes=[pltpu.VMEM((tg, d), F32), pltpu.SemaphoreType.DMA(())],
    )
    return pl.pallas_call(
        body, grid_spec=grid_spec,
        out_shape=jax.ShapeDtypeStruct((rows, d), BF16),
        compiler_params=_params(("arbitrary",)),
        name=name,
    )(src_tok, n_valid, h)


def _gu_body(it_ref, ic_ref, ie_ref, if_ref, ot_ref, oc_ref, ni_ref, x_ref, wg_ref, wu_ref,
             bg_ref, bu_ref, act_ref, wgb_ref, wub_ref):
    s = pl.program_id(0)

    @pl.when(s >= ni_ref[0])
    def _():
        act_ref[...] = jnp.zeros(act_ref.shape, act_ref.dtype)

    @pl.when(s < ni_ref[0])
    def _():
        @pl.when(if_ref[s] == 1)
        def _():
            wgb_ref[...] = wg_ref[...].astype(BF16)
            wub_ref[...] = wu_ref[...].astype(BF16)

        x = x_ref[...]
        gate = jnp.minimum(_dot(x, wgb_ref[...]) + bg_ref[...], SWIGLU_LIMIT)
        up = jnp.clip(_dot(x, wub_ref[...]) + bu_ref[...], -SWIGLU_LIMIT, SWIGLU_LIMIT)
        act = (up + 1.0) * gate * jax.nn.sigmoid(SWIGLU_ALPHA * gate)
        act_ref[...] = act.astype(act_ref.dtype)


def _moe_gate_up(xg, w_gu, b_gu, items, *, name="moe_gu"):
    rows, d = xg.shape
    n_exp, _, two_e = w_gu.shape
    e_dim = two_e // 2
    tc = MOE_TC
    ncb = e_dim // tc
    tm = MOE_TM
    it, ic, ie, ifirst, ot, oc, n_items = items
    grid_spec = pltpu.PrefetchScalarGridSpec(
        num_scalar_prefetch=7,
        grid=(it.shape[0],),
        in_specs=[
            pl.BlockSpec((tm, d), lambda s, it, ic, ie, f, ot, oc, ni: (it[s], 0)),
            pl.BlockSpec((None, d, tc), lambda s, it, ic, ie, f, ot, oc, ni: (ie[s], 0, ic[s])),
            pl.BlockSpec((None, d, tc), lambda s, it, ic, ie, f, ot, oc, ni: (ie[s], 0, ncb + ic[s])),
            pl.BlockSpec((None, 1, tc), lambda s, it, ic, ie, f, ot, oc, ni: (ie[s], 0, ic[s])),
            pl.BlockSpec((None, 1, tc), lambda s, it, ic, ie, f, ot, oc, ni: (ie[s], 0, ncb + ic[s])),
        ],
        out_specs=pl.BlockSpec((tm, tc), lambda s, it, ic, ie, f, ot, oc, ni: (ot[s], oc[s])),
        scratch_shapes=[pltpu.VMEM((d, tc), BF16), pltpu.VMEM((d, tc), BF16)],
    )
    return pl.pallas_call(
        _gu_body, grid_spec=grid_spec,
        out_shape=jax.ShapeDtypeStruct((rows, e_dim), BF16),
        compiler_params=_params(("arbitrary",)),
        name=name,
    )(it, ic, ie, ifirst, ot, oc, n_items, xg, w_gu, w_gu, b_gu.reshape(n_exp, 1, two_e),
      b_gu.reshape(n_exp, 1, two_e))


def _down_body(te_ref, tf_ref, nv_ref, act_ref, wd_ref, bd_ref, g_ref, y_ref, wdb_ref):
    i = pl.program_id(0)

    @pl.when(i < nv_ref[0])
    def _():
        @pl.when(tf_ref[i] == 1)
        def _():
            wdb_ref[...] = wd_ref[...].astype(BF16)

        y = _dot(act_ref[...], wdb_ref[...]) + bd_ref[...]
        y_ref[...] = g_ref[...] * y

    @pl.when(i >= nv_ref[0])
    def _():
        y_ref[...] = jnp.zeros(y_ref.shape, y_ref.dtype)


def _moe_down(act, w_down, b_down, row_gate, tile_expert, tile_first, n_valid, *, name="moe_down"):
    rows, e_dim = act.shape
    n_exp, _, d = w_down.shape
    tm = MOE_TM
    grid_spec = pltpu.PrefetchScalarGridSpec(
        num_scalar_prefetch=3,
        grid=(rows // tm,),
        in_specs=[
            pl.BlockSpec((tm, e_dim), lambda i, te, tf, nv: (i, 0)),
            pl.BlockSpec((None, e_dim, d), lambda i, te, tf, nv: (te[i], 0, 0)),
            pl.BlockSpec((None, 1, d), lambda i, te, tf, nv: (te[i], 0, 0)),
            pl.BlockSpec((tm, 1), lambda i, te, tf, nv: (i, 0)),
        ],
        out_specs=pl.BlockSpec((tm, d), lambda i, te, tf, nv: (i, 0)),
        scratch_shapes=[pltpu.VMEM((e_dim, d), BF16)],
    )
    return pl.pallas_call(
        _down_body, grid_spec=grid_spec,
        out_shape=jax.ShapeDtypeStruct((rows, d), F32),
        compiler_params=_params(("arbitrary",)),
        name=name,
    )(tile_expert, tile_first, n_valid, act, w_down, b_down.reshape(n_exp, 1, d), row_gate)


def _combine_body(pos_ref, y_hbm, x_ref, gate_ref, o_ref, buf_ref, sem, *, tt):
    i = pl.program_id(0)
    base = i * tt * TOP_K

    def issue(r, c):
        for k in range(TOP_K):
            p = pos_ref[base + r * TOP_K + k]
            pltpu.make_async_copy(y_hbm.at[pl.ds(p, 1), :], buf_ref.at[k, pl.ds(r, 1), :], sem).start()
        return c

    lax.fori_loop(0, tt, issue, 0)
    for k in range(TOP_K):
        pltpu.make_async_copy(y_hbm.at[pl.ds(0, tt), :], buf_ref.at[k], sem).wait()
    moe = (buf_ref[0] + buf_ref[1]) + (buf_ref[2] + buf_ref[3])
    o_ref[...] = x_ref[...] + gate_ref[...] * moe


def _moe_combine(yg, pos, x, gate, *, name="moe_combine"):
    n, d = x.shape
    tt = min(COMBINE_TT, n)
    g_groups, g_rows, _ = gate.shape
    bpg = (n // tt) // g_groups
    body = functools.partial(_combine_body, tt=tt)
    grid_spec = pltpu.PrefetchScalarGridSpec(
        num_scalar_prefetch=1,
        grid=(n // tt,),
        in_specs=[
            pl.BlockSpec(memory_space=pl.ANY),
            pl.BlockSpec((tt, d), lambda i, p: (i, 0)),
            pl.BlockSpec((None, g_rows, d), lambda i, p: (i // bpg, 0, 0)),
        ],
        out_specs=pl.BlockSpec((tt, d), lambda i, p: (i, 0)),
        scratch_shapes=[pltpu.VMEM((TOP_K, tt, d), F32), pltpu.SemaphoreType.DMA(())],
    )
    return pl.pallas_call(
        body, grid_spec=grid_spec,
        out_shape=jax.ShapeDtypeStruct((n, d), F32),
        compiler_params=_params(("arbitrary",)),
        name=name,
    )(pos, yg, x, gate)


def _routing_tables(top_idx, top_gate, *, rows, e_dim):
    n_tok = top_idx.shape[0]
    n_slots = n_tok * TOP_K
    tm = MOE_TM
    n_tiles = rows // tm
    e_flat = top_idx.reshape(n_slots)
    slot_ids = jnp.arange(n_slots, dtype=jnp.int32)
    order = jnp.sort(e_flat * n_slots + slot_ids) % n_slots
    e_sorted = e_flat[order]
    experts = jnp.arange(N_EXPERTS, dtype=jnp.int32)
    counts = jnp.sum((e_flat[:, None] == experts[None, :]).astype(jnp.int32), axis=0)
    c_end = jnp.cumsum(counts).astype(jnp.int32)
    c_start = c_end - counts
    tiles_per = (counts + tm - 1) // tm
    t_end = jnp.cumsum(tiles_per).astype(jnp.int32)
    t_start = t_end - tiles_per
    n_valid = t_end[-1]

    rank = slot_ids - c_start[e_sorted]
    grow_sorted = t_start[e_sorted] * tm + rank
    inv = jnp.sort(order * n_slots + slot_ids) % n_slots
    pos = grow_sorted[inv]

    tile_ids = jnp.arange(n_tiles, dtype=jnp.int32)
    tile_expert = jnp.minimum(
        jnp.sum((tile_ids[:, None] >= t_end[None, :]).astype(jnp.int32), axis=1), N_EXPERTS - 1)
    last_valid_expert = tile_expert[jnp.maximum(n_valid - 1, 0)]
    tile_expert = jnp.where(tile_ids < n_valid, tile_expert, last_valid_expert)
    tile_first = ((tile_ids == t_start[tile_expert]) & (tile_ids < n_valid)).astype(jnp.int32)

    r = jnp.arange(rows, dtype=jnp.int32)
    r_exp = tile_expert[r // tm]
    local = r - t_start[r_exp] * tm
    live = (local < counts[r_exp]) & (r // tm < n_valid)
    src_slot = order[jnp.clip(c_start[r_exp] + local, 0, n_slots - 1)]
    src_tok = jnp.where(live, src_slot // TOP_K, 0).astype(jnp.int32)
    row_gate = jnp.where(live, top_gate.reshape(n_slots)[src_slot], 0.0).reshape(rows, 1)

    ncb = e_dim // MOE_TC
    n_items_max = n_tiles * ncb
    s = jnp.arange(n_items_max, dtype=jnp.int32)
    item_end = t_end * ncb
    s_exp = jnp.minimum(
        jnp.sum((s[:, None] >= item_end[None, :]).astype(jnp.int32), axis=1), N_EXPERTS - 1)
    s_local = s - t_start[s_exp] * ncb
    s_tiles = jnp.maximum(tiles_per[s_exp], 1)
    s_chunk = s_local // s_tiles
    s_tile = t_start[s_exp] + s_local % s_tiles
    n_items = n_valid * ncb
    ok = s < n_items
    last = jnp.maximum(n_items - 1, 0)
    spare = s - n_items
    o_tile = jnp.where(ok, s_tile, n_valid + spare // ncb).astype(jnp.int32)
    o_chunk = jnp.where(ok, s_chunk, spare % ncb).astype(jnp.int32)
    s_exp = jnp.where(ok, s_exp, s_exp[last])
    s_chunk = jnp.where(ok, s_chunk, s_chunk[last]).astype(jnp.int32)
    s_tile = jnp.where(ok, s_tile, s_tile[last]).astype(jnp.int32)
    s_first = (ok & (s_local % s_tiles == 0)).astype(jnp.int32)
    items = (s_tile, s_chunk, s_exp, s_first, o_tile, o_chunk, n_items.reshape(1))
    return pos, src_tok, row_gate, tile_expert, tile_first, n_valid.reshape(1), items


def _in_proj_groups(h, wi, conv_ch, att_w, qn_g, kn_g, act_dtype):
    c0 = 2 * conv_ch
    offs = [c0 + i * att_w for i in range(6)]
    u = _matmul([h], wi, col_off=0, n_cols=c0, name="proj_conv")
    dq = _matmul([h], wi, col_off=offs[0], n_cols=att_w, gain=qn_g, out_dtype=act_dtype, name="proj_dq")
    dk = _matmul([h], wi, col_off=offs[1], n_cols=att_w, gain=kn_g, name="proj_dk")
    dv = _matmul([h], wi, col_off=offs[2], n_cols=att_w, name="proj_dv")
    sq = _matmul([h], wi, col_off=offs[3], n_cols=att_w, out_dtype=act_dtype, name="proj_sq")
    sk = _matmul([h], wi, col_off=offs[4], n_cols=att_w, name="proj_sk")
    sv = _matmul([h], wi, col_off=offs[5], n_cols=att_w, name="proj_sv")
    return u, dq, dk, dv, sq, sk, sv


def kernel(x_prompt, x_sample, cache_diff_k, cache_diff_v, cache_sb_k, cache_sb_v, state_conv, page_table, c_prompt, c_sample, ada_w, ada_b, ada_table, norm1_g, norm2_g, w_in, w_out, conv_w, conv_b, conv_ln_g, conv_ln_b, diff_qn_g, diff_kn_g, diff_lambda, diff_subln_g, router_w, router_b, moe_w_gate_up, moe_b_gate_up, moe_w_down, moe_b_down):
    bp, tp, d = x_prompt.shape
    bs, ts, _ = x_sample.shape
    depth = w_in.shape[0]
    conv_ch = conv_w.shape[2]
    att_w = (w_in.shape[2] - 2 * conv_ch) // 6
    diff_heads = att_w // (2 * HEAD_DIM)
    n_p, n_s = bp * tp, bs * ts
    n_tok = n_p + n_s
    moe_rows = -(-(n_tok * TOP_K + N_EXPERTS * (MOE_TM - 1)) // MOE_TM) * MOE_TM

    c_all = jnp.concatenate([c_prompt, c_sample], axis=0)
    c_rows = -(-c_all.shape[0] // 16) * 16
    silu_c = jnp.pad(jax.nn.silu(c_all), ((0, c_rows - c_all.shape[0]), (0, 0)))
    mod_all = _matmul([silu_c], ada_w, bias=ada_b, name="ada_mod")
    mod_p = mod_all[:bp].reshape(bp, N_MOD, d)
    mod_s = mod_all[bp:bp + bs].reshape(bs, N_MOD, d)

    slopes = jnp.repeat(2.0 ** (-8.0 * (jnp.arange(diff_heads, dtype=F32) + 1.0) / diff_heads), 2)
    conv0 = jnp.zeros((bp, CONV_K - 1, conv_ch), F32)

    xp = x_prompt.reshape(n_p, d)
    xs = x_sample.reshape(n_s, d)
    st_p = [[] for _ in range(5)]
    st_s = [[] for _ in range(5)]
    for l in range(depth):
        lam_init = 0.8 - 0.6 * math.exp(-0.3 * l)
        lp = diff_lambda[l].astype(F32)
        lam = jnp.exp(jnp.sum(lp[0] * lp[1])) - jnp.exp(jnp.sum(lp[2] * lp[3])) + lam_init
        scalars = jnp.concatenate([lam.reshape(1), slopes]).astype(F32)
        table = ada_table[l]
        gate1_p = mod_p[:, 2:3, :] + table[2][None, None, :]
        gate2_p = mod_p[:, 5:6, :] + table[5][None, None, :]
        gate1_s = jnp.repeat(mod_s[:, 2, :] + table[2][None, :], ts, axis=0)[None]
        gate2_s = jnp.repeat(mod_s[:, 5, :] + table[5][None, :], ts, axis=0)[None]

        h = _norm_mod(xp.reshape(bp, tp, d), norm1_g[l], mod_p, table, i_shift=0, i_scale=1,
                      out_dtype=BF16, tt=NORM_TT, name="norm1_p")
        u, dq, dk, dv, sq, sk, sv = _in_proj_groups(h, w_in[l], conv_ch, att_w,
                                                    diff_qn_g[l], diff_kn_g[l], BF16)
        y_conv, new_conv = _conv_module(u, conv0, conv_w[l], conv_b[l], conv_ln_g[l], conv_ln_b[l],
                                        batch=bp, tt=CONV_TT, out_dtype=BF16, name="conv_p")
        o_d = _diff_attn_prompt(dq, dk, dv, scalars, diff_subln_g[l], batch=bp, lam_init=lam_init)
        o_s = _sb_attn_prompt(sq, sk, sv, batch=bp)
        xp = _matmul([y_conv, o_d, o_s], w_out[l], resid=xp, gate=gate1_p, name="out_proj_p")
        for lst, val in zip(st_p, (dk, dv, sk, sv, new_conv)):
            lst.append(val)

        h = _norm_mod(xs.reshape(bs, ts, d), norm1_g[l], mod_s, table, i_shift=0, i_scale=1,
                      out_dtype=F32, tt=ts, name="norm1_s")
        u, dq, dk, dv, sq, sk, sv = _in_proj_groups(h, w_in[l], conv_ch, att_w,
                                                    diff_qn_g[l], diff_kn_g[l], F32)
        y_conv, new_conv = _conv_module(u, state_conv[l], conv_w[l], conv_b[l], conv_ln_g[l],
                                        conv_ln_b[l], batch=bs, tt=ts, out_dtype=F32, name="conv_s")
        o_d = _diff_attn_sample(dq, dk, dv, cache_diff_k, cache_diff_v, page_table, scalars,
                                diff_subln_g[l], layer=l, lam_init=lam_init)
        o_s = _sb_attn_sample(sq, sk, sv, cache_sb_k, cache_sb_v, page_table, layer=l)
        xs = _matmul([y_conv, o_d, o_s], w_out[l], resid=xs, gate=gate1_s, name="out_proj_s")
        for lst, val in zip(st_s, (dk, dv, sk, sv, new_conv)):
            lst.append(val)

        router = (router_w[l], router_b[l])
        h2_p, idx_p, gt_p = _norm_mod(xp.reshape(bp, tp, d), norm2_g[l], mod_p, table, i_shift=3,
                                      i_scale=4, out_dtype=F32, tt=NORM_TT, router=router,
                                      name="norm2_p")
        h2_s, idx_s, gt_s = _norm_mod(xs.reshape(bs, ts, d), norm2_g[l], mod_s, table, i_shift=3,
                                      i_scale=4, out_dtype=F32, tt=ts, router=router,
                                      name="norm2_s")
        h2 = jnp.concatenate([h2_p, h2_s], axis=0)
        top_idx = jnp.concatenate([idx_p[:, :TOP_K], idx_s[:, :TOP_K]], axis=0)
        top_gate = jnp.concatenate([gt_p[:, :TOP_K], gt_s[:, :TOP_K]], axis=0)
        pos, src_tok, row_gate, tile_expert, tile_first, n_valid, items = _routing_tables(
            top_idx, top_gate, rows=moe_rows, e_dim=moe_w_down.shape[2])
        xg = _moe_gather(h2, src_tok, n_valid, rows=moe_rows)
        act = _moe_gate_up(xg, moe_w_gate_up[l], moe_b_gate_up[l], items)
        yg = _moe_down(act, moe_w_down[l], moe_b_down[l], row_gate, tile_expert, tile_first, n_valid)
        xp = _moe_combine(yg, pos[:n_p * TOP_K], xp, gate2_p, name="moe_combine_p")
        xs = _moe_combine(yg, pos[n_p * TOP_K:], xs, gate2_s, name="moe_combine_s")

    def stack(vals, batch, shape):
        return jnp.stack(vals).reshape((depth, batch) + shape)

    dkh = (2 * diff_heads, HEAD_DIM)
    dvh = (diff_heads, 2 * HEAD_DIM)
    sbh = (att_w // HEAD_DIM, HEAD_DIM)
    outs_p = [stack(st_p[0], bp, (tp,) + dkh), stack(st_p[1], bp, (tp,) + dvh),
              stack(st_p[2], bp, (tp,) + sbh), stack(st_p[3], bp, (tp,) + sbh),
              jnp.stack(st_p[4])]
    outs_s = [stack(st_s[0], bs, (ts,) + dkh), stack(st_s[1], bs, (ts,) + dvh),
              stack(st_s[2], bs, (ts,) + sbh), stack(st_s[3], bs, (ts,) + sbh),
              jnp.stack(st_s[4])]
    return (xp.reshape(bp, tp, d), xs.reshape(bs, ts, d), *outs_p, *outs_s)
```

```python
import functools
import math

import jax
import jax.numpy as jnp
from jax import lax
from jax.experimental import pallas as pl
from jax.experimental.pallas import tpu as pltpu

F32 = jnp.float32
BF16 = jnp.bfloat16

HEAD_DIM = 128
N_MOD = 6
CONV_K = 31
N_EXPERTS = 32
TOP_K = 4
SWIGLU_LIMIT = 7.0
SWIGLU_ALPHA = 1.702
EPS = 1e-6
PAGE_SIZE = 128
SCALE = HEAD_DIM ** -0.5
NEG_INF = float("-inf")

LANES = 128
VMEM_LIMIT = 56 * 1024 * 1024

MM_TM = 1024
MM_TN = 512
NORM_TT = 256
CONV_TT = 256
CONV_RB = 32
ATT_TQ = 512
PAGES_PER_STEP = 4
MOE_TM = 256
MOE_TC = 384
COMBINE_TT = 128
GATHER_UNROLL = 8


def _params(sem, vmem=VMEM_LIMIT):
    return pltpu.CompilerParams(dimension_semantics=sem, vmem_limit_bytes=vmem)


def _dot(a, b):
    return jnp.dot(a, b, preferred_element_type=F32)


def _dot_nt(a, b):
    return lax.dot_general(a, b, (((1,), (1,)), ((), ())), preferred_element_type=F32)


def _rms(x, gain):
    y = x * lax.rsqrt(jnp.mean(x * x, axis=-1, keepdims=True) + EPS)
    return y * gain


def _mm_body(*refs, k_splits, has_bias, has_gain, has_resid, head_w):
    n_a = len(k_splits)
    a_refs = refs[:n_a]
    w_ref = refs[n_a]
    pos = n_a + 1
    bias_ref = gain_ref = x_ref = gate_ref = None
    if has_bias:
        bias_ref = refs[pos]
        pos += 1
    if has_gain:
        gain_ref = refs[pos]
        pos += 1
    if has_resid:
        x_ref, gate_ref = refs[pos], refs[pos + 1]
        pos += 2
    o_ref = refs[pos]

    acc = None
    off = 0
    for a_ref, kk in zip(a_refs, k_splits):
        part = _dot(a_ref[...].astype(BF16), w_ref[off:off + kk, :].astype(BF16))
        acc = part if acc is None else acc + part
        off += kk
    if has_bias:
        acc = acc + bias_ref[...]
    if has_gain:
        g = gain_ref[...]
        pieces = []
        for c in range(acc.shape[1] // HEAD_DIM):
            pieces.append(_rms(acc[:, c * HEAD_DIM:(c + 1) * HEAD_DIM], g))
        acc = jnp.concatenate(pieces, axis=1)
    if has_resid:
        acc = x_ref[...] + gate_ref[...] * acc
    if head_w is None:
        o_ref[...] = acc.astype(o_ref.dtype)
    else:
        bb, hpb, tr, _ = o_ref.shape
        for c in range(hpb):
            piece = acc[:, c * head_w:(c + 1) * head_w]
            o_ref[:, c] = piece.reshape(bb, tr, head_w).astype(o_ref.dtype)


def _matmul(a_list, w, *, layer=0, col_off=0, n_cols=None, bias=None, gain=None, resid=None,
            gate=None, out_dtype=F32, heads_out=None, name="mm"):
    m = a_list[0].shape[0]
    k_splits = tuple(a.shape[1] for a in a_list)
    k_total = sum(k_splits)
    assert w.shape[1] == k_total
    n_cols = w.shape[2] - col_off if n_cols is None else n_cols
    tm = min(MM_TM, m)
    tn = MM_TN
    assert m % tm == 0 and n_cols % tn == 0 and col_off % tn == 0
    joff = col_off // tn
    grid = (m // tm, n_cols // tn)

    in_specs = [pl.BlockSpec((tm, kk), lambda i, j: (i, 0)) for kk in k_splits]
    in_specs.append(pl.BlockSpec((None, k_total, tn), lambda i, j: (layer, 0, j + joff)))
    args = list(a_list) + [w]
    if bias is not None:
        in_specs.append(pl.BlockSpec((1, tn), lambda i, j: (0, j)))
        args.append(bias.reshape(1, n_cols))
    if gain is not None:
        in_specs.append(pl.BlockSpec((1, HEAD_DIM), lambda i, j: (0, 0)))
        args.append(gain.reshape(1, HEAD_DIM))
    if resid is not None:
        g_groups, g_rows, _ = gate.shape
        bpg = (m // tm) // g_groups
        in_specs.append(pl.BlockSpec((tm, tn), lambda i, j: (i, j)))
        in_specs.append(pl.BlockSpec((None, g_rows, tn), lambda i, j: (i // bpg, 0, j)))
        args += [resid, gate]

    if heads_out is None:
        head_w = None
        out_spec = pl.BlockSpec((tm, tn), lambda i, j: (i, j))
        out_shape = jax.ShapeDtypeStruct((m, n_cols), out_dtype)
    else:
        batch, head_w = heads_out
        t = m // batch
        hpb = tn // head_w
        tr = min(tm, t)
        bb = tm // tr
        ipb = t // tr
        assert tn % head_w == 0 and tm % tr == 0 and t % tr == 0 and (bb == 1 or ipb == 1)
        out_spec = pl.BlockSpec((bb, hpb, tr, head_w), lambda i, j: (i // ipb, j, i % ipb, 0))
        out_shape = jax.ShapeDtypeStruct((batch, n_cols // head_w, t, head_w), out_dtype)

    body = functools.partial(_mm_body, k_splits=k_splits, has_bias=bias is not None,
                             has_gain=gain is not None, has_resid=resid is not None,
                             head_w=head_w)
    return pl.pallas_call(
        body,
        grid=grid,
        in_specs=in_specs,
        out_specs=out_spec,
        out_shape=out_shape,
        compiler_params=_params(("parallel", "arbitrary")),
        name=name,
    )(*args)


def _norm_body(x_ref, g_ref, mod_ref, tab_ref, *refs, i_shift, i_scale, with_router):
    if with_router:
        rw_ref, rb_ref, h_ref, idx_ref, gate_ref = refs
    else:
        (h_ref,) = refs
    x = x_ref[...]
    mod = mod_ref[...] + tab_ref[...]
    shift = mod[i_shift:i_shift + 1, :]
    scale = mod[i_scale:i_scale + 1, :]
    h = _rms(x, g_ref[...]) * (1.0 + scale) + shift
    h_ref[...] = h.astype(h_ref.dtype)
    if not with_router:
        return
    w = rw_ref[...]
    h_hi = h.astype(BF16)
    h_lo = (h - h_hi.astype(F32)).astype(BF16)
    w_hi = w.astype(BF16)
    w_lo = (w - w_hi.astype(F32)).astype(BF16)
    logits = _dot(h_hi, w_hi) + (_dot(h_hi, w_lo) + _dot(h_lo, w_hi)) + rb_ref[...]
    tt = logits.shape[0]
    e_iota = lax.broadcasted_iota(jnp.int32, (tt, N_EXPERTS), 1).astype(F32)
    lane = lax.broadcasted_iota(jnp.int32, (tt, LANES), 1)
    idx_out = jnp.zeros((tt, LANES), jnp.int32)
    val_out = jnp.zeros((tt, LANES), F32)
    work = logits
    vals = []
    for k in range(TOP_K):
        v = jnp.max(work, axis=-1, keepdims=True)
        sel = jnp.min(jnp.where(work == v, e_iota, float(N_EXPERTS)), axis=-1, keepdims=True)
        work = jnp.where(e_iota == sel, NEG_INF, work)
        idx_out = jnp.where(lane == k, sel.astype(jnp.int32), idx_out)
        vals.append(v)
    ex = [jnp.exp(v - vals[0]) for v in vals]
    den = ex[0] + ex[1] + ex[2] + ex[3]
    for k in range(TOP_K):
        val_out = jnp.where(lane == k, ex[k] / den, val_out)
    idx_ref[...] = idx_out
    gate_ref[...] = val_out


def _norm_mod(x, g, mod, table, *, i_shift, i_scale, out_dtype, tt, router=None, name="norm"):
    b, t, d = x.shape
    nt = t // tt
    assert t % tt == 0
    in_specs = [
        pl.BlockSpec((None, tt, d), lambda bi, ti: (bi, ti, 0)),
        pl.BlockSpec((1, d), lambda bi, ti: (0, 0)),
        pl.BlockSpec((None, N_MOD, d), lambda bi, ti: (bi, 0, 0)),
        pl.BlockSpec((N_MOD, d), lambda bi, ti: (0, 0)),
    ]
    args = [x, g.reshape(1, d), mod, table]
    row_spec = lambda w: pl.BlockSpec((tt, w), lambda bi, ti: (bi * nt + ti, 0))
    out_specs = [row_spec(d)]
    out_shape = [jax.ShapeDtypeStruct((b * t, d), out_dtype)]
    if router is not None:
        rw, rb = router
        in_specs += [pl.BlockSpec((d, N_EXPERTS), lambda bi, ti: (0, 0)),
                     pl.BlockSpec((1, N_EXPERTS), lambda bi, ti: (0, 0))]
        args += [rw, rb.reshape(1, N_EXPERTS)]
        out_specs += [row_spec(LANES), row_spec(LANES)]
        out_shape += [jax.ShapeDtypeStruct((b * t, LANES), jnp.int32),
                      jax.ShapeDtypeStruct((b * t, LANES), F32)]
    body = functools.partial(_norm_body, i_shift=i_shift, i_scale=i_scale,
                             with_router=router is not None)
    outs = pl.pallas_call(
        body, grid=(b, nt), in_specs=in_specs, out_specs=out_specs, out_shape=out_shape,
        compiler_params=_params(("parallel", "arbitrary")), name=name,
    )(*args)
    return outs if router is not None else outs[0]


HIST = CONV_K - 1
HIST_PAD = 32


def _conv_body(u_ref, st_ref, w_ref, b_ref, lg_ref, lb_ref, y_ref, ns_ref, ext_ref, *, tt, ch):
    ti = pl.program_id(1)
    lead = HIST_PAD - HIST

    @pl.when(ti == 0)
    def _():
        ext_ref[lead:HIST_PAD, :] = st_ref[...]

    @pl.when(ti > 0)
    def _():
        ext_ref[lead:HIST_PAD, :] = ext_ref[tt + lead:tt + HIST_PAD, :]

    u = u_ref[...]
    ext_ref[HIST_PAD:HIST_PAD + tt, :] = u[:, :ch] * jax.nn.sigmoid(u[:, ch:])

    rb = min(CONV_RB, tt)
    for r0 in range(0, tt, rb):
        acc = jnp.broadcast_to(b_ref[...], (rb, ch))
        for k in range(CONV_K):
            acc = acc + w_ref[k:k + 1, :] * ext_ref[r0 + lead + k:r0 + lead + k + rb, :]
        mu = jnp.mean(acc, axis=-1, keepdims=True)
        cen = acc - mu
        var = jnp.mean(cen * cen, axis=-1, keepdims=True)
        yn = cen * lax.rsqrt(var + EPS) * lg_ref[...] + lb_ref[...]
        y_ref[r0:r0 + rb, :] = (yn * jax.nn.sigmoid(yn)).astype(y_ref.dtype)

    @pl.when(ti == pl.num_programs(1) - 1)
    def _():
        ns_ref[...] = ext_ref[tt + lead:tt + HIST_PAD, :]


def _conv_module(u, state, cw, cb, lg, lb, *, batch, tt, out_dtype, name="conv"):
    n, c2 = u.shape
    ch = c2 // 2
    t = n // batch
    nt = t // tt
    assert t % tt == 0 and (nt == 1 or tt >= HIST)
    body = functools.partial(_conv_body, tt=tt, ch=ch)
    vec = lambda: pl.BlockSpec((1, ch), lambda bi, ti: (0, 0))
    return pl.pallas_call(
        body,
        grid=(batch, nt),
        in_specs=[
            pl.BlockSpec((tt, c2), lambda bi, ti: (bi * nt + ti, 0)),
            pl.BlockSpec((None, HIST, ch), lambda bi, ti: (bi, 0, 0)),
            pl.BlockSpec((CONV_K, ch), lambda bi, ti: (0, 0)),
            vec(), vec(), vec(),
        ],
        out_specs=[
            pl.BlockSpec((tt, ch), lambda bi, ti: (bi * nt + ti, 0)),
            pl.BlockSpec((None, HIST, ch), lambda bi, ti: (bi, 0, 0)),
        ],
        out_shape=[jax.ShapeDtypeStruct((n, ch), out_dtype),
                   jax.ShapeDtypeStruct((batch, HIST, ch), F32)],
        scratch_shapes=[pltpu.VMEM((HIST_PAD + tt, ch), F32)],
        compiler_params=_params(("parallel", "arbitrary")),
        name=name,
    )(u, state, cw, cb.reshape(1, ch), lg.reshape(1, ch), lb.reshape(1, ch))


def _subln(o, sg, lam_init):
    return _rms(o, sg) * (1.0 - lam_init)


def _diff_p_body(sc_ref, q_ref, k_ref, v_ref, sg_ref, o_ref,
                 kb_ref, vb_ref, m_ref, l_ref, acc_ref, *, tq, lam_init):
    hd = pl.program_id(1)
    t = q_ref.shape[0]
    nq = t // tq
    kb_ref[...] = (k_ref[...] * SCALE).astype(BF16)
    vb_ref[...] = v_ref[...].astype(BF16)
    lam = sc_ref[0]
    row = lax.broadcasted_iota(jnp.int32, (tq, tq), 0)
    col = lax.broadcasted_iota(jnp.int32, (tq, tq), 1)
    d0 = (row - col).astype(F32)

    def update(sub, s, koff):
        m_old = m_ref[sub]
        m_new = jnp.maximum(m_old, jnp.max(s, axis=-1, keepdims=True))
        alpha = jnp.exp(m_old - m_new)
        p = jnp.exp(s - m_new)
        l_ref[sub] = alpha * l_ref[sub] + jnp.sum(p, axis=-1, keepdims=True)
        acc_ref[sub] = alpha * acc_ref[sub] + _dot(p.astype(BF16), vb_ref[pl.ds(koff, tq), :])
        m_ref[sub] = m_new

    def q_block(qi, carry):
        qoff = pl.multiple_of(qi * tq, tq)
        q = [q_ref[pl.ds(qoff, tq), sub * HEAD_DIM:(sub + 1) * HEAD_DIM] for sub in range(2)]
        slope = [sc_ref[1 + 2 * hd + sub] for sub in range(2)]
        sd0 = [slope[sub] * d0 for sub in range(2)]
        m_ref[...] = jnp.full(m_ref.shape, NEG_INF, F32)
        l_ref[...] = jnp.zeros(l_ref.shape, F32)
        acc_ref[...] = jnp.zeros(acc_ref.shape, F32)

        def k_block(kj, c):
            koff = pl.multiple_of(kj * tq, tq)
            gap = lax.convert_element_type((qi - kj) * tq, F32)
            for sub in range(2):
                s = _dot_nt(q[sub], kb_ref[sub, pl.ds(koff, tq), :])
                update(sub, s - (sd0[sub] + slope[sub] * gap), koff)
            return c

        lax.fori_loop(0, qi, k_block, 0)
        for sub in range(2):
            s = _dot_nt(q[sub], kb_ref[sub, pl.ds(qoff, tq), :])
            s = jnp.where(d0 >= 0, s - sd0[sub], NEG_INF)
            update(sub, s, qoff)
        o = acc_ref[0] / l_ref[0] - lam * (acc_ref[1] / l_ref[1])
        o_ref[pl.ds(qoff, tq), :] = _subln(o, sg_ref[...], lam_init).astype(o_ref.dtype)
        return carry

    lax.fori_loop(0, nq, q_block, 0)


def _diff_attn_prompt(q, k, v, scalars, sg, *, lam_init, name="diff_p"):
    batch, heads, t, hw = v.shape
    n, width = q.shape
    tq = min(ATT_TQ, t)
    body = functools.partial(_diff_p_body, tq=tq, lam_init=lam_init)
    return pl.pallas_call(
        body,
        grid=(batch, heads),
        in_specs=[pl.BlockSpec(memory_space=pltpu.SMEM),
                  pl.BlockSpec((t, hw), lambda b, h: (b, h)),
                  pl.BlockSpec((None, 2, t, HEAD_DIM), lambda b, h: (b, h, 0, 0)),
                  pl.BlockSpec((None, None, t, hw), lambda b, h: (b, h, 0, 0)),
                  pl.BlockSpec((1, hw), lambda b, h: (0, 0))],
        out_specs=pl.BlockSpec((t, hw), lambda b, h: (b, h)),
        out_shape=jax.ShapeDtypeStruct((n, width), BF16),
        scratch_shapes=[pltpu.VMEM((2, t, HEAD_DIM), BF16), pltpu.VMEM((t, hw), BF16),
                        pltpu.VMEM((2, tq, 1), F32), pltpu.VMEM((2, tq, 1), F32),
                        pltpu.VMEM((2, tq, hw), F32)],
        compiler_params=_params(("parallel", "arbitrary")),
        name=name,
    )(scalars, q, k, v, sg.reshape(1, hw))


def _softplus(z):
    return jnp.maximum(z, 0.0) + jnp.log(1.0 + jnp.exp(-jnp.abs(z)))


def _suffix_matrix(n):
    r = lax.broadcasted_iota(jnp.int32, (n, n), 0)
    c = lax.broadcasted_iota(jnp.int32, (n, n), 1)
    return (r > c).astype(BF16)


def _suffix_sum(x, u):
    return _dot(x.astype(BF16), u)


SB_HEADS_PER_STEP = 2


def _sb_p_body(q_ref, k_ref, v_ref, o_ref, kb_ref, vb_ref, c_ref, acc_ref, *, tq):
    t = q_ref.shape[0]
    nq = t // tq
    nh = SB_HEADS_PER_STEP
    kb_ref[...] = (k_ref[...] * SCALE).astype(BF16)
    vb_ref[...] = v_ref[...].astype(BF16)
    row = lax.broadcasted_iota(jnp.int32, (tq, tq), 0)
    col = lax.broadcasted_iota(jnp.int32, (tq, tq), 1)
    below = col < row
    u = _suffix_matrix(tq)

    def block(q, koff, mask):
        for h in range(nh):
            z = _dot_nt(q[h], kb_ref[h, pl.ds(koff, tq), :])
            sp = _softplus(z)
            log_not = -sp if mask is None else jnp.where(mask, -sp, 0.0)
            between = _suffix_sum(log_not, u) + c_ref[h]
            a = jnp.exp((z - sp) + between)
            if mask is not None:
                a = jnp.where(mask, a, 0.0)
            acc_ref[h] += _dot(a.astype(BF16), vb_ref[h, pl.ds(koff, tq), :])
            c_ref[h] += jnp.sum(log_not, axis=-1, keepdims=True)

    def q_block(qi, carry):
        qoff = pl.multiple_of(qi * tq, tq)
        q = [q_ref[pl.ds(qoff, tq), h * HEAD_DIM:(h + 1) * HEAD_DIM] for h in range(nh)]
        c_ref[...] = jnp.zeros(c_ref.shape, F32)
        acc_ref[...] = jnp.zeros(acc_ref.shape, F32)
        block(q, qoff, below)

        def k_block(jj, c):
            koff = pl.multiple_of((qi - 1 - jj) * tq, tq)
            block(q, koff, None)
            return c

        lax.fori_loop(0, qi, k_block, 0)
        for h in range(nh):
            o_ref[pl.ds(qoff, tq), h * HEAD_DIM:(h + 1) * HEAD_DIM] = acc_ref[h].astype(o_ref.dtype)
        return carry

    lax.fori_loop(0, nq, q_block, 0)


def _sb_attn_prompt(q, k, v, *, name="sb_p"):
    batch, heads, t, _ = k.shape
    n, width = q.shape
    nh = SB_HEADS_PER_STEP
    tq = min(ATT_TQ, t)
    rows = lambda: pl.BlockSpec((t, nh * HEAD_DIM), lambda b, h: (b, h))
    hm = lambda: pl.BlockSpec((None, nh, t, HEAD_DIM), lambda b, h: (b, h, 0, 0))
    body = functools.partial(_sb_p_body, tq=tq)
    return pl.pallas_call(
        body,
        grid=(batch, heads // nh),
        in_specs=[rows(), hm(), hm()],
        out_specs=rows(),
        out_shape=jax.ShapeDtypeStruct((n, width), BF16),
        scratch_shapes=[pltpu.VMEM((nh, t, HEAD_DIM), BF16), pltpu.VMEM((nh, t, HEAD_DIM), BF16),
                        pltpu.VMEM((nh, tq, 1), F32), pltpu.VMEM((nh, tq, HEAD_DIM), F32)],
        compiler_params=_params(("parallel", "arbitrary")),
        name=name,
    )(q, k, v)


def _pad_new(x):
    t = x.shape[2]
    return jnp.pad(x, ((0, 0), (0, 0), (0, PAGE_SIZE - t), (0, 0)))


def _diff_s_body(pt_ref, sc_ref, q_ref, kn_ref, vn_ref, *refs, n_sub, tnew, past_len, lam_init, pps):
    k_refs = refs[:pps]
    v_refs = refs[pps:2 * pps]
    sg_ref, o_ref, m_ref, l_ref, acc_ref = refs[2 * pps:]
    j = pl.program_id(1)
    n_steps = pl.num_programs(1)
    row = lax.broadcasted_iota(jnp.int32, (tnew, PAGE_SIZE), 0)
    col = lax.broadcasted_iota(jnp.int32, (tnew, PAGE_SIZE), 1)
    q_all = q_ref[...] * SCALE

    def q_of(h):
        return q_all[:, h * HEAD_DIM:(h + 1) * HEAD_DIM].astype(BF16)

    rows_of = lambda x, h: x[h * tnew:(h + 1) * tnew]

    @pl.when(j == 0)
    def _():
        d_new = (row - col).astype(F32)
        s_heads = []
        for h in range(n_sub):
            s = _dot_nt(q_of(h), kn_ref[h].astype(BF16)) - sc_ref[1 + h] * d_new
            s_heads.append(jnp.where(col <= row, s, NEG_INF))
        s_all = jnp.concatenate(s_heads, axis=0)
        m_new = jnp.max(s_all, axis=-1, keepdims=True)
        pr = jnp.exp(s_all - m_new)
        m_ref[...] = m_new
        l_ref[...] = jnp.sum(pr, axis=-1, keepdims=True)
        acc_ref[...] = jnp.concatenate(
            [_dot(rows_of(pr, h).astype(BF16), vn_ref[h // 2].astype(BF16)) for h in range(n_sub)],
            axis=0)

    d_base = (past_len + row - col).astype(F32)
    s_heads = []
    for h in range(n_sub):
        slope = sc_ref[1 + h]
        sd = slope * d_base
        q_h = q_of(h)
        parts = []
        for p in range(pps):
            shift = slope * lax.convert_element_type((j * pps + p) * PAGE_SIZE, F32)
            parts.append(_dot_nt(q_h, k_refs[p][h].astype(BF16)) - (sd - shift))
        s_heads.append(jnp.concatenate(parts, axis=1))
    s_all = jnp.concatenate(s_heads, axis=0)
    m_old = m_ref[...]
    m_new = jnp.maximum(m_old, jnp.max(s_all, axis=-1, keepdims=True))
    alpha = jnp.exp(m_old - m_new)
    pr = jnp.exp(s_all - m_new)
    l_ref[...] = alpha * l_ref[...] + jnp.sum(pr, axis=-1, keepdims=True)
    m_ref[...] = m_new
    pv_heads = []
    for hd in range(n_sub // 2):
        v_bf = [v_refs[p][hd].astype(BF16) for p in range(pps)]
        for sub in range(2):
            pr_h = rows_of(pr, 2 * hd + sub)
            pv = None
            for p in range(pps):
                part = _dot(pr_h[:, p * PAGE_SIZE:(p + 1) * PAGE_SIZE].astype(BF16), v_bf[p])
                pv = part if pv is None else pv + part
            pv_heads.append(pv)
    acc_ref[...] = alpha * acc_ref[...] + jnp.concatenate(pv_heads, axis=0)

    @pl.when(j == n_steps - 1)
    def _():
        lam = sc_ref[0]
        o_all = acc_ref[...] / l_ref[...]
        for hd in range(n_sub // 2):
            o = rows_of(o_all, 2 * hd) - lam * rows_of(o_all, 2 * hd + 1)
            o = _subln(o, sg_ref[...], lam_init)
            o_ref[:, hd * 2 * HEAD_DIM:(hd + 1) * 2 * HEAD_DIM] = o.astype(o_ref.dtype)


def _diff_attn_sample(q, k_new, v_new, cache_k, cache_v, page_table, scalars, sg, *,
                      layer, lam_init, name="diff_s"):
    batch, n_pages = page_table.shape
    n, width = q.shape
    tnew = n // batch
    n_sub = width // HEAD_DIM
    hw = 2 * HEAD_DIM
    pps = PAGES_PER_STEP
    assert n_pages % pps == 0
    body = functools.partial(_diff_s_body, n_sub=n_sub, tnew=tnew, past_len=n_pages * PAGE_SIZE,
                             lam_init=lam_init, pps=pps)

    def page_spec(heads, dim, p):
        return pl.BlockSpec((None, None, heads, PAGE_SIZE, dim),
                            lambda b, j, pt: (layer, pt[b, j * pps + p], 0, 0, 0))

    grid_spec = pltpu.PrefetchScalarGridSpec(
        num_scalar_prefetch=1,
        grid=(batch, n_pages // pps),
        in_specs=[
            pl.BlockSpec(memory_space=pltpu.SMEM),
            pl.BlockSpec((tnew, width), lambda b, j, pt: (b, 0)),
            pl.BlockSpec((None, n_sub, PAGE_SIZE, HEAD_DIM), lambda b, j, pt: (b, 0, 0, 0)),
            pl.BlockSpec((None, n_sub // 2, PAGE_SIZE, hw), lambda b, j, pt: (b, 0, 0, 0)),
            *[page_spec(n_sub, HEAD_DIM, p) for p in range(pps)],
            *[page_spec(n_sub // 2, hw, p) for p in range(pps)],
            pl.BlockSpec((1, hw), lambda b, j, pt: (0, 0)),
        ],
        out_specs=pl.BlockSpec((tnew, width), lambda b, j, pt: (b, 0)),
        scratch_shapes=[pltpu.VMEM((n_sub * tnew, 1), F32), pltpu.VMEM((n_sub * tnew, 1), F32),
                        pltpu.VMEM((n_sub * tnew, hw), F32)],
    )
    return pl.pallas_call(
        body, grid_spec=grid_spec,
        out_shape=jax.ShapeDtypeStruct((n, width), F32),
        compiler_params=_params(("parallel", "arbitrary")),
        name=name,
    )(page_table, scalars, q, _pad_new(k_new), _pad_new(v_new),
      *([cache_k] * pps), *([cache_v] * pps), sg.reshape(1, hw))


def _sb_s_body(pt_ref, q_ref, kn_ref, vn_ref, *refs, heads, tnew, pps):
    k_refs = refs[:pps]
    v_refs = refs[pps:2 * pps]
    o_ref, c_ref, acc_ref = refs[2 * pps:]
    j = pl.program_id(1)
    n_steps = pl.num_programs(1)
    row = jnp.concatenate([lax.broadcasted_iota(jnp.int32, (tnew, PAGE_SIZE), 0)] * heads, axis=0)
    col = lax.broadcasted_iota(jnp.int32, (heads * tnew, PAGE_SIZE), 1)
    u = _suffix_matrix(PAGE_SIZE)
    q_all = q_ref[...] * SCALE
    q_bf = [q_all[:, h * HEAD_DIM:(h + 1) * HEAD_DIM].astype(BF16) for h in range(heads)]

    def group(kv_refs, mask):
        z = [jnp.concatenate([_dot_nt(q_bf[h], k_ref[h].astype(BF16)) for h in range(heads)], axis=0)
             for k_ref, _ in kv_refs]
        sp = [_softplus(zp) for zp in z]
        ln = [-s if mask is None else jnp.where(mask, -s, 0.0) for s in sp]
        suffix = [_suffix_sum(x, u) for x in ln]
        total = [jnp.sum(x, axis=-1, keepdims=True) for x in ln]
        c = c_ref[...]
        a_blocks = []
        for p in range(len(kv_refs)):
            a = jnp.exp((z[p] - sp[p]) + (suffix[p] + c))
            a_blocks.append(a if mask is None else jnp.where(mask, a, 0.0))
            c = c + total[p]
        c_ref[...] = c
        pv_heads = []
        for h in range(heads):
            pv = None
            for p, (_, v_ref) in enumerate(kv_refs):
                part = _dot(a_blocks[p][h * tnew:(h + 1) * tnew].astype(BF16), v_ref[h].astype(BF16))
                pv = part if pv is None else pv + part
            pv_heads.append(pv)
        acc_ref[...] += jnp.concatenate(pv_heads, axis=0)

    @pl.when(j == 0)
    def _():
        c_ref[...] = jnp.zeros(c_ref.shape, F32)
        acc_ref[...] = jnp.zeros(acc_ref.shape, F32)
        group([(kn_ref, vn_ref)], col < row)

    group([(k_refs[p], v_refs[p]) for p in range(pps)], None)

    @pl.when(j == n_steps - 1)
    def _():
        for h in range(heads):
            o_ref[:, h * HEAD_DIM:(h + 1) * HEAD_DIM] = (
                acc_ref[h * tnew:(h + 1) * tnew, :].astype(o_ref.dtype))


def _sb_attn_sample(q, k_new, v_new, cache_k, cache_v, page_table, *, layer, name="sb_s"):
    batch, n_pages = page_table.shape
    n, width = q.shape
    tnew = n // batch
    heads = width // HEAD_DIM
    pps = PAGES_PER_STEP
    assert n_pages % pps == 0
    body = functools.partial(_sb_s_body, heads=heads, tnew=tnew, pps=pps)

    def page_spec(p):
        return pl.BlockSpec(
            (None, None, heads, PAGE_SIZE, HEAD_DIM),
            lambda b, j, pt: (layer, pt[b, n_pages - 1 - (j * pps + p)], 0, 0, 0))

    new_spec = lambda: pl.BlockSpec((None, heads, PAGE_SIZE, HEAD_DIM), lambda b, j, pt: (b, 0, 0, 0))
    grid_spec = pltpu.PrefetchScalarGridSpec(
        num_scalar_prefetch=1,
        grid=(batch, n_pages // pps),
        in_specs=[
            pl.BlockSpec((tnew, width), lambda b, j, pt: (b, 0)),
            new_spec(), new_spec(),
            *[page_spec(p) for p in range(pps)],
            *[page_spec(p) for p in range(pps)],
        ],
        out_specs=pl.BlockSpec((tnew, width), lambda b, j, pt: (b, 0)),
        scratch_shapes=[pltpu.VMEM((heads * tnew, 1), F32),
                        pltpu.VMEM((heads * tnew, HEAD_DIM), F32)],
    )
    return pl.pallas_call(
        body, grid_spec=grid_spec,
        out_shape=jax.ShapeDtypeStruct((n, width), F32),
        compiler_params=_params(("parallel", "arbitrary")),
        name=name,
    )(page_table, q, _pad_new(k_new), _pad_new(v_new), *([cache_k] * pps), *([cache_v] * pps))


def _gather_body(tok_ref, off_ref, nv_ref, hp_hbm, hs_hbm, o_ref, buf_ref, sem, *, tg, n_p):
    i = pl.program_id(0)
    nv = nv_ref[0]

    def issue_tile(tile, slot):
        base = off_ref[tile]

        def issue(r, c):
            tok = tok_ref[base + r]
            dst = buf_ref.at[slot, pl.ds(r, 1), :]

            @pl.when(tok < n_p)
            def _():
                pltpu.make_async_copy(hp_hbm.at[pl.ds(tok, 1), :], dst, sem.at[slot]).start()

            @pl.when(tok >= n_p)
            def _():
                pltpu.make_async_copy(hs_hbm.at[pl.ds(tok - n_p, 1), :], dst, sem.at[slot]).start()

            return c

        lax.fori_loop(0, tg, issue, 0, unroll=GATHER_UNROLL)

    slot = i % 2

    @pl.when((i == 0) & (nv > 0))
    def _():
        issue_tile(0, 0)

    @pl.when(i + 1 < nv)
    def _():
        issue_tile(i + 1, 1 - slot)

    @pl.when(i < nv)
    def _():
        pltpu.make_async_copy(hp_hbm.at[pl.ds(0, tg), :], buf_ref.at[slot], sem.at[slot]).wait()
        o_ref[...] = buf_ref[slot].astype(o_ref.dtype)

    @pl.when(i >= nv)
    def _():
        o_ref[...] = jnp.zeros(o_ref.shape, o_ref.dtype)


def _moe_gather(h_p, h_s, order, tile_off, n_valid, *, rows, name="moe_gather"):
    n_p, d = h_p.shape
    tg = MOE_TM
    tok_sorted = jnp.pad(order // TOP_K, (0, tg))
    body = functools.partial(_gather_body, tg=tg, n_p=n_p)
    grid_spec = pltpu.PrefetchScalarGridSpec(
        num_scalar_prefetch=3,
        grid=(rows // tg,),
        in_specs=[pl.BlockSpec(memory_space=pl.ANY), pl.BlockSpec(memory_space=pl.ANY)],
        out_specs=pl.BlockSpec((tg, d), lambda i, o, f, nv: (i, 0)),
        scratch_shapes=[pltpu.VMEM((2, tg, d), F32), pltpu.SemaphoreType.DMA((2,))],
    )
    return pl.pallas_call(
        body, grid_spec=grid_spec,
        out_shape=jax.ShapeDtypeStruct((rows, d), BF16),
        compiler_params=_params(("arbitrary",)),
        name=name,
    )(tok_sorted, tile_off, n_valid, h_p, h_s)


def _gu_body(it_ref, ic_ref, ie_ref, if_ref, ot_ref, oc_ref, ni_ref, x_ref, wg_ref, wu_ref,
             bg_ref, bu_ref, act_ref, wgb_ref, wub_ref):
    s = pl.program_id(0)

    @pl.when(s >= ni_ref[0])
    def _():
        act_ref[...] = jnp.zeros(act_ref.shape, act_ref.dtype)

    @pl.when(s < ni_ref[0])
    def _():
        @pl.when(if_ref[s] == 1)
        def _():
            wgb_ref[...] = wg_ref[...].astype(BF16)
            wub_ref[...] = wu_ref[...].astype(BF16)

        x = x_ref[...]
        gate = jnp.minimum(_dot(x, wgb_ref[...]) + bg_ref[...], SWIGLU_LIMIT)
        up = jnp.clip(_dot(x, wub_ref[...]) + bu_ref[...], -SWIGLU_LIMIT, SWIGLU_LIMIT)
        act = (up + 1.0) * gate * jax.nn.sigmoid(SWIGLU_ALPHA * gate)
        act_ref[...] = act.astype(act_ref.dtype)


def _moe_gate_up(xg, w_gu, b_gu, items, *, layer, name="moe_gu"):
    rows, d = xg.shape
    _, n_exp, _, two_e = w_gu.shape
    e_dim = two_e // 2
    tc = MOE_TC
    ncb = e_dim // tc
    tm = MOE_TM
    it, ic, ie, ifirst, ot, oc, n_items = items
    b4 = b_gu.reshape(b_gu.shape[0], n_exp, 1, two_e)
    wspec = lambda up: pl.BlockSpec(
        (None, None, d, tc),
        lambda s, it, ic, ie, f, ot, oc, ni: (layer, ie[s], 0, up * ncb + ic[s]))
    bspec = lambda up: pl.BlockSpec(
        (None, None, 1, tc),
        lambda s, it, ic, ie, f, ot, oc, ni: (layer, ie[s], 0, up * ncb + ic[s]))
    grid_spec = pltpu.PrefetchScalarGridSpec(
        num_scalar_prefetch=7,
        grid=(it.shape[0],),
        in_specs=[
            pl.BlockSpec((tm, d), lambda s, it, ic, ie, f, ot, oc, ni: (it[s], 0)),
            wspec(0), wspec(1), bspec(0), bspec(1),
        ],
        out_specs=pl.BlockSpec((tm, tc), lambda s, it, ic, ie, f, ot, oc, ni: (ot[s], oc[s])),
        scratch_shapes=[pltpu.VMEM((d, tc), BF16), pltpu.VMEM((d, tc), BF16)],
    )
    return pl.pallas_call(
        _gu_body, grid_spec=grid_spec,
        out_shape=jax.ShapeDtypeStruct((rows, e_dim), BF16),
        compiler_params=_params(("arbitrary",)),
        name=name,
    )(it, ic, ie, ifirst, ot, oc, n_items, xg, w_gu, w_gu, b4, b4)


def _down_body(te_ref, tf_ref, nv_ref, act_ref, wd_ref, bd_ref, y_ref, wdb_ref):
    i = pl.program_id(0)

    @pl.when(i < nv_ref[0])
    def _():
        @pl.when(tf_ref[i] == 1)
        def _():
            wdb_ref[...] = wd_ref[...].astype(BF16)

        y_ref[...] = _dot(act_ref[...], wdb_ref[...]) + bd_ref[...]

    @pl.when(i >= nv_ref[0])
    def _():
        y_ref[...] = jnp.zeros(y_ref.shape, y_ref.dtype)


def _moe_down(act, w_down, b_down, tile_expert, tile_first, n_valid, *, layer, name="moe_down"):
    rows, e_dim = act.shape
    _, n_exp, _, d = w_down.shape
    tm = MOE_TM
    b4 = b_down.reshape(b_down.shape[0], n_exp, 1, d)
    grid_spec = pltpu.PrefetchScalarGridSpec(
        num_scalar_prefetch=3,
        grid=(rows // tm,),
        in_specs=[
            pl.BlockSpec((tm, e_dim), lambda i, te, tf, nv: (i, 0)),
            pl.BlockSpec((None, None, e_dim, d), lambda i, te, tf, nv: (layer, te[i], 0, 0)),
            pl.BlockSpec((None, None, 1, d), lambda i, te, tf, nv: (layer, te[i], 0, 0)),
        ],
        out_specs=pl.BlockSpec((tm, d), lambda i, te, tf, nv: (i, 0)),
        scratch_shapes=[pltpu.VMEM((e_dim, d), BF16)],
    )
    return pl.pallas_call(
        _down_body, grid_spec=grid_spec,
        out_shape=jax.ShapeDtypeStruct((rows, d), F32),
        compiler_params=_params(("arbitrary",)),
        name=name,
    )(tile_expert, tile_first, n_valid, act, w_down, b4)


def _combine_body(pos_ref, y_hbm, x_ref, gate_ref, tg_ref, o_ref, buf_ref, sem, *, tt):
    i = pl.program_id(0)
    base = i * tt * TOP_K

    def issue(r, c):
        for k in range(TOP_K):
            p = pos_ref[base + r * TOP_K + k]
            pltpu.make_async_copy(y_hbm.at[pl.ds(p, 1), :], buf_ref.at[k, pl.ds(r, 1), :], sem).start()
        return c

    lax.fori_loop(0, tt, issue, 0, unroll=GATHER_UNROLL // 2)
    for k in range(TOP_K):
        pltpu.make_async_copy(y_hbm.at[pl.ds(0, tt), :], buf_ref.at[k], sem).wait()
    tg = tg_ref[...]
    w = [tg[:, k:k + 1] for k in range(TOP_K)]
    moe = (w[0] * buf_ref[0] + w[1] * buf_ref[1]) + (w[2] * buf_ref[2] + w[3] * buf_ref[3])
    o_ref[...] = x_ref[...] + gate_ref[...] * moe


def _moe_combine(yg, pos, top_gate, x, gate, *, name="moe_combine"):
    n, d = x.shape
    tt = min(COMBINE_TT, n)
    g_groups, g_rows, _ = gate.shape
    bpg = (n // tt) // g_groups
    body = functools.partial(_combine_body, tt=tt)
    grid_spec = pltpu.PrefetchScalarGridSpec(
        num_scalar_prefetch=1,
        grid=(n // tt,),
        in_specs=[
            pl.BlockSpec(memory_space=pl.ANY),
            pl.BlockSpec((tt, d), lambda i, p: (i, 0)),
            pl.BlockSpec((None, g_rows, d), lambda i, p: (i // bpg, 0, 0)),
            pl.BlockSpec((tt, LANES), lambda i, p: (i, 0)),
        ],
        out_specs=pl.BlockSpec((tt, d), lambda i, p: (i, 0)),
        scratch_shapes=[pltpu.VMEM((TOP_K, tt, d), F32), pltpu.SemaphoreType.DMA(())],
    )
    return pl.pallas_call(
        body, grid_spec=grid_spec,
        out_shape=jax.ShapeDtypeStruct((n, d), F32),
        compiler_params=_params(("arbitrary",)),
        name=name,
    )(pos, yg, x, gate, top_gate)


def _lookup(table, idx):
    ids = jnp.arange(table.shape[0], dtype=jnp.int32)
    return jnp.sum(jnp.where(idx[:, None] == ids[None, :], table[None, :], 0), axis=1)


def _routing_tables(top_idx, *, rows, e_dim):
    n_tok = top_idx.shape[0]
    n_slots = n_tok * TOP_K
    tm = MOE_TM
    n_tiles = rows // tm
    assert n_slots < 2 ** 16 and rows < 2 ** 16 and N_EXPERTS * n_slots < 2 ** 31
    e_flat = top_idx.reshape(n_slots)
    slot_ids = jnp.arange(n_slots, dtype=jnp.int32)
    keys = jnp.sort(e_flat * n_slots + slot_ids)
    order = keys % n_slots
    e_sorted = keys // n_slots
    experts = jnp.arange(N_EXPERTS, dtype=jnp.int32)
    counts = jnp.sum((e_flat[:, None] == experts[None, :]).astype(jnp.int32), axis=0)
    c_end = jnp.cumsum(counts).astype(jnp.int32)
    c_start = c_end - counts
    tiles_per = (counts + tm - 1) // tm
    t_end = jnp.cumsum(tiles_per).astype(jnp.int32)
    t_start = t_end - tiles_per
    n_valid = t_end[-1]

    grow_sorted = slot_ids + _lookup(t_start * tm - c_start, e_sorted)
    packed = order.astype(jnp.uint32) * jnp.uint32(2 ** 16) + grow_sorted.astype(jnp.uint32)
    pos = (jnp.sort(packed) % jnp.uint32(2 ** 16)).astype(jnp.int32)

    tile_ids = jnp.arange(n_tiles, dtype=jnp.int32)
    tile_expert = jnp.minimum(
        jnp.sum((tile_ids[:, None] >= t_end[None, :]).astype(jnp.int32), axis=1), N_EXPERTS - 1)
    last_valid_expert = tile_expert[jnp.maximum(n_valid - 1, 0)]
    valid_tile = tile_ids < n_valid
    tile_expert = jnp.where(valid_tile, tile_expert, last_valid_expert)
    tile_first = ((tile_ids == t_start[tile_expert]) & valid_tile).astype(jnp.int32)
    tile_local = (tile_ids - t_start[tile_expert]) * tm
    tile_off = jnp.where(valid_tile, c_start[tile_expert] + tile_local, 0).astype(jnp.int32)

    ncb = e_dim // MOE_TC
    n_items_max = n_tiles * ncb
    s = jnp.arange(n_items_max, dtype=jnp.int32)
    item_end = t_end * ncb
    s_exp = jnp.minimum(
        jnp.sum((s[:, None] >= item_end[None, :]).astype(jnp.int32), axis=1), N_EXPERTS - 1)
    s_local = s - t_start[s_exp] * ncb
    s_tiles = jnp.maximum(tiles_per[s_exp], 1)
    s_chunk = s_local // s_tiles
    s_tile = t_start[s_exp] + s_local % s_tiles
    n_items = n_valid * ncb
    ok = s < n_items
    last = jnp.maximum(n_items - 1, 0)
    spare = s - n_items
    o_tile = jnp.where(ok, s_tile, n_valid + spare // ncb).astype(jnp.int32)
    o_chunk = jnp.where(ok, s_chunk, spare % ncb).astype(jnp.int32)
    s_exp = jnp.where(ok, s_exp, s_exp[last])
    s_chunk = jnp.where(ok, s_chunk, s_chunk[last]).astype(jnp.int32)
    s_tile = jnp.where(ok, s_tile, s_tile[last]).astype(jnp.int32)
    s_first = (ok & (s_local % s_tiles == 0)).astype(jnp.int32)
    items = (s_tile, s_chunk, s_exp, s_first, o_tile, o_chunk, n_items.reshape(1))
    return pos, order, tile_off, tile_expert, tile_first, n_valid.reshape(1), items


def _in_proj_groups(h, w_in, layer, batch, conv_ch, att_w, qn_g, kn_g, act_dtype):
    c0 = 2 * conv_ch
    offs = [c0 + i * att_w for i in range(6)]
    mm = functools.partial(_matmul, [h], w_in, layer=layer, n_cols=att_w)
    u = _matmul([h], w_in, layer=layer, col_off=0, n_cols=c0, name="proj_conv")
    dq = mm(col_off=offs[0], gain=qn_g, out_dtype=act_dtype, name="proj_dq")
    dk = mm(col_off=offs[1], gain=kn_g, heads_out=(batch, HEAD_DIM), name="proj_dk")
    dv = mm(col_off=offs[2], heads_out=(batch, 2 * HEAD_DIM), name="proj_dv")
    sq = mm(col_off=offs[3], out_dtype=act_dtype, name="proj_sq")
    sk = mm(col_off=offs[4], heads_out=(batch, HEAD_DIM), name="proj_sk")
    sv = mm(col_off=offs[5], heads_out=(batch, HEAD_DIM), name="proj_sv")
    return u, dq, dk, dv, sq, sk, sv


def _head_major(cache):
    return jnp.transpose(cache, (0, 1, 3, 2, 4))


def kernel(x_prompt, x_sample, cache_diff_k, cache_diff_v, cache_sb_k, cache_sb_v, state_conv, page_table, c_prompt, c_sample, ada_w, ada_b, ada_table, norm1_g, norm2_g, w_in, w_out, conv_w, conv_b, conv_ln_g, conv_ln_b, diff_qn_g, diff_kn_g, diff_lambda, diff_subln_g, router_w, router_b, moe_w_gate_up, moe_b_gate_up, moe_w_down, moe_b_down):
    bp, tp, d = x_prompt.shape
    bs, ts, _ = x_sample.shape
    depth = w_in.shape[0]
    conv_ch = conv_w.shape[2]
    att_w = (w_in.shape[2] - 2 * conv_ch) // 6
    diff_heads = att_w // (2 * HEAD_DIM)
    e_dim = moe_w_down.shape[2]
    n_p, n_s = bp * tp, bs * ts
    n_tok = n_p + n_s
    moe_rows = -(-(n_tok * TOP_K + N_EXPERTS * (MOE_TM - 1)) // MOE_TM) * MOE_TM

    ck_d, cv_d, ck_s, cv_s = (_head_major(c) for c in (cache_diff_k, cache_diff_v, cache_sb_k, cache_sb_v))

    c_all = jnp.concatenate([c_prompt, c_sample], axis=0)
    c_rows = -(-c_all.shape[0] // 16) * 16
    silu_c = jnp.pad(jax.nn.silu(c_all), ((0, c_rows - c_all.shape[0]), (0, 0)))
    mod_all = _matmul([silu_c], ada_w[None], bias=ada_b, name="ada_mod")
    mod_p = mod_all[:bp].reshape(bp, N_MOD, d)
    mod_s = mod_all[bp:bp + bs].reshape(bs, N_MOD, d)

    slopes = jnp.repeat(2.0 ** (-8.0 * (jnp.arange(diff_heads, dtype=F32) + 1.0) / diff_heads), 2)
    conv0 = jnp.zeros((bp, CONV_K - 1, conv_ch), F32)

    xp = x_prompt.reshape(n_p, d)
    xs = x_sample.reshape(n_s, d)
    st_p = [[] for _ in range(5)]
    st_s = [[] for _ in range(5)]
    for l in range(depth):
        lam_init = 0.8 - 0.6 * math.exp(-0.3 * l)
        lp = diff_lambda[l].astype(F32)
        lam = jnp.exp(jnp.sum(lp[0] * lp[1])) - jnp.exp(jnp.sum(lp[2] * lp[3])) + lam_init
        scalars = jnp.concatenate([lam.reshape(1), slopes]).astype(F32)
        table = ada_table[l]
        gate1_p = mod_p[:, 2:3, :] + table[2][None, None, :]
        gate2_p = mod_p[:, 5:6, :] + table[5][None, None, :]
        gate1_s = jnp.repeat(mod_s[:, 2, :] + table[2][None, :], ts, axis=0)[None]
        gate2_s = jnp.repeat(mod_s[:, 5, :] + table[5][None, :], ts, axis=0)[None]

        h = _norm_mod(xp.reshape(bp, tp, d), norm1_g[l], mod_p, table, i_shift=0, i_scale=1,
                      out_dtype=BF16, tt=NORM_TT, name="norm1_p")
        u, dq, dk, dv, sq, sk, sv = _in_proj_groups(h, w_in, l, bp, conv_ch, att_w,
                                                    diff_qn_g[l], diff_kn_g[l], BF16)
        y_conv, new_conv = _conv_module(u, conv0, conv_w[l], conv_b[l], conv_ln_g[l], conv_ln_b[l],
                                        batch=bp, tt=CONV_TT, out_dtype=BF16, name="conv_p")
        o_d = _diff_attn_prompt(dq, dk, dv, scalars, diff_subln_g[l], lam_init=lam_init)
        o_s = _sb_attn_prompt(sq, sk, sv)
        xp = _matmul([y_conv, o_d, o_s], w_out, layer=l, resid=xp, gate=gate1_p, name="out_proj_p")
        for lst, val in zip(st_p, (dk, dv, sk, sv, new_conv)):
            lst.append(val)

        h = _norm_mod(xs.reshape(bs, ts, d), norm1_g[l], mod_s, table, i_shift=0, i_scale=1,
                      out_dtype=F32, tt=ts, name="norm1_s")
        u, dq, dk, dv, sq, sk, sv = _in_proj_groups(h, w_in, l, bs, conv_ch, att_w,
                                                    diff_qn_g[l], diff_kn_g[l], F32)
        y_conv, new_conv = _conv_module(u, state_conv[l], conv_w[l], conv_b[l], conv_ln_g[l],
                                        conv_ln_b[l], batch=bs, tt=ts, out_dtype=F32, name="conv_s")
        o_d = _diff_attn_sample(dq, dk, dv, ck_d, cv_d, page_table, scalars,
                                diff_subln_g[l], layer=l, lam_init=lam_init)
        o_s = _sb_attn_sample(sq, sk, sv, ck_s, cv_s, page_table, layer=l)
        xs = _matmul([y_conv, o_d, o_s], w_out, layer=l, resid=xs, gate=gate1_s, name="out_proj_s")
        for lst, val in zip(st_s, (dk, dv, sk, sv, new_conv)):
            lst.append(val)

        router = (router_w[l], router_b[l])
        h2_p, idx_p, gt_p = _norm_mod(xp.reshape(bp, tp, d), norm2_g[l], mod_p, table, i_shift=3,
                                      i_scale=4, out_dtype=F32, tt=NORM_TT, router=router,
                                      name="norm2_p")
        h2_s, idx_s, gt_s = _norm_mod(xs.reshape(bs, ts, d), norm2_g[l], mod_s, table, i_shift=3,
                                      i_scale=4, out_dtype=F32, tt=ts, router=router,
                                      name="norm2_s")
        top_idx = jnp.concatenate([idx_p[:, :TOP_K], idx_s[:, :TOP_K]], axis=0)
        pos, order, tile_off, tile_expert, tile_first, n_valid, items = _routing_tables(
            top_idx, rows=moe_rows, e_dim=e_dim)
        xg = _moe_gather(h2_p, h2_s, order, tile_off, n_valid, rows=moe_rows)
        act = _moe_gate_up(xg, moe_w_gate_up, moe_b_gate_up, items, layer=l)
        yg = _moe_down(act, moe_w_down, moe_b_down, tile_expert, tile_first, n_valid, layer=l)
        xp = _moe_combine(yg, pos[:n_p * TOP_K], gt_p, xp, gate2_p, name="moe_combine_p")
        xs = _moe_combine(yg, pos[n_p * TOP_K:], gt_s, xs, gate2_s, name="moe_combine_s")

    def stack(vals):
        return jnp.transpose(jnp.stack(vals), (0, 1, 3, 2, 4))

    outs_p = [stack(st_p[0]), stack(st_p[1]), stack(st_p[2]), stack(st_p[3]), jnp.stack(st_p[4])]
    outs_s = [stack(st_s[0]), stack(st_s[1]), stack(st_s[2]), stack(st_s[3]), jnp.stack(st_s[4])]
    return (xp.reshape(bp, tp, d), xs.reshape(bs, ts, d), *outs_p, *outs_s)
```
